```python
import jax, jax.numpy as jnp
from jax import lax
import numpy as np

D_MODEL = 1024
BATCH = 2
SEQ = 16384
DEPTH = 1

CHUNK = 64
LEFT_CHUNKS = 8
HEAD_DIM = 64
N_HEADS_A = 8
N_HEADS_B = 8
WIDTH_A = N_HEADS_A * HEAD_DIM
WIDTH_B = N_HEADS_B * HEAD_DIM
MAX_REL = 128
N_REL = 2 * MAX_REL + 1
QUERY_BLOCK = 128
D_FF = 2816
CONV_WIDTH = 3
IN_WIDTH = 3 * WIDTH_A + 3 * WIDTH_B + 2 * D_MODEL
EPS = 1e-6

kernel_name = "hybrid_chunked_stickbreaking_convglu_block"


def rms_norm(x, gain):
    xf = x.astype(jnp.float32)
    y = xf * lax.rsqrt(jnp.mean(xf * xf, axis=-1, keepdims=True) + EPS)
    return (y * gain.astype(jnp.float32)).astype(x.dtype)


def chunked_attention(q, k, v, q_gain, k_gain, rel_bias):
    b, s, h, dh = q.shape
    nc = s // CHUNK
    band = (LEFT_CHUNKS + 1) * CHUNK
    q = rms_norm(q, q_gain)
    k = rms_norm(k, k_gain)
    qc = q.reshape(b, nc, CHUNK, h, dh)
    pad = ((0, 0), (LEFT_CHUNKS * CHUNK, 0), (0, 0), (0, 0))
    kp = jnp.pad(k, pad).reshape(b, nc + LEFT_CHUNKS, CHUNK, h, dh)
    vp = jnp.pad(v, pad).reshape(b, nc + LEFT_CHUNKS, CHUNK, h, dh)
    kb = jnp.concatenate([kp[:, i:i + nc] for i in range(LEFT_CHUNKS + 1)], axis=2)
    vb = jnp.concatenate([vp[:, i:i + nc] for i in range(LEFT_CHUNKS + 1)], axis=2)
    scores = jnp.einsum('bnqhd,bnkhd->bnhqk', qc, kb).astype(jnp.float32) * (dh ** -0.5)
    r = jnp.arange(CHUNK)[:, None]
    key_off = jnp.arange(band)[None, :] - LEFT_CHUNKS * CHUNK
    rel = r - key_off
    idx = jnp.clip(rel, -MAX_REL, MAX_REL) + MAX_REL
    bias = rel_bias[:, idx].astype(jnp.float32)
    valid = (jnp.arange(nc)[:, None] * CHUNK + key_off) >= 0
    scores = jnp.where(valid[None, :, None, None, :], scores + bias[None, None], -jnp.inf)
    probs = jax.nn.softmax(scores, axis=-1).astype(v.dtype)
    out = jnp.einsum('bnhqk,bnkhd->bnqhd', probs, vb)
    return out.reshape(b, s, h * dh)


def stick_breaking_attention(q, k, v):
    b, s, h, dh = q.shape
    nb = s // QUERY_BLOCK
    kh = k.transpose(0, 2, 1, 3)
    vh = v.transpose(0, 2, 1, 3)
    qblocks = q.transpose(0, 2, 1, 3).reshape(b, h, nb, QUERY_BLOCK, dh).transpose(2, 0, 1, 3, 4)
    key_pos = jnp.arange(s)

    def block(args):
        qb, blk = args
        z = jnp.einsum('bhqd,bhkd->bhqk', qb, kh).astype(jnp.float32) * (dh ** -0.5)
        q_pos = blk * QUERY_BLOCK + jnp.arange(QUERY_BLOCK)
        causal = key_pos[None, :] < q_pos[:, None]
        log_keep = jnp.where(causal, -jax.nn.softplus(z), 0.0)
        log_after = lax.cumsum(log_keep, axis=3, reverse=True) - log_keep
        w = jnp.where(causal, jnp.exp(jax.nn.log_sigmoid(z) + log_after), 0.0)
        return jnp.einsum('bhqk,bhkd->bhqd', w.astype(vh.dtype), vh)

    out = lax.map(block, (qblocks, jnp.arange(nb)))
    return out.transpose(1, 0, 3, 2, 4).reshape(b, s, h * dh)


def conv_glu(x, w_up, conv_w, conv_b, w_down):
    hid = x @ w_up
    hid = lax.conv_general_dilated(
        hid, conv_w[:, None, :], window_strides=(1,),
        padding=((CONV_WIDTH - 1, 0),),
        dimension_numbers=('NWC', 'WIO', 'NWC'),
        feature_group_count=2 * D_FF) + conv_b
    gate, up = hid[..., :D_FF], hid[..., D_FF:]
    return (jax.nn.silu(gate) * up) @ w_down


def setup_inputs(seed: int = 0) -> dict:
    key = jax.random.key(seed)
    ks = jax.random.split(key, 16)
    f32 = jnp.float32
    nrm = lambda k, shape, scale: jax.random.normal(k, shape, f32) * scale
    return {
        "x": nrm(ks[0], (BATCH, SEQ, D_MODEL), 1.0),
        "norm1_g": 1.0 + nrm(ks[1], (DEPTH, D_MODEL), 0.02),
        "w_in": nrm(ks[2], (DEPTH, D_MODEL, IN_WIDTH), D_MODEL ** -0.5),
        "q_norm_g": 1.0 + nrm(ks[3], (DEPTH, HEAD_DIM), 0.02),
        "k_norm_g": 1.0 + nrm(ks[4], (DEPTH, HEAD_DIM), 0.02),
        "rel_bias": nrm(ks[5], (DEPTH, N_HEADS_A, N_REL), 0.2),
        "w_branch_a": nrm(ks[6], (DEPTH, WIDTH_A, D_MODEL), WIDTH_A ** -0.5),
        "w_branch_b": nrm(ks[7], (DEPTH, WIDTH_B, D_MODEL), WIDTH_B ** -0.5),
        "w_out": nrm(ks[8], (DEPTH, D_MODEL, D_MODEL), D_MODEL ** -0.5),
        "norm2_g": 1.0 + nrm(ks[9], (DEPTH, D_MODEL), 0.02),
        "w_ffn_up": nrm(ks[10], (DEPTH, D_MODEL, 2 * D_FF), D_MODEL ** -0.5),
        "ffn_conv_w": nrm(ks[11], (DEPTH, CONV_WIDTH, 2 * D_FF), CONV_WIDTH ** -0.5),
        "ffn_conv_b": nrm(ks[12], (DEPTH, 2 * D_FF), 0.01),
        "w_ffn_down": nrm(ks[13], (DEPTH, D_FF, D_MODEL), D_FF ** -0.5),
    }


def reference(x, norm1_g, w_in, q_norm_g, k_norm_g, rel_bias, w_branch_a, w_branch_b,
              w_out, norm2_g, w_ffn_up, ffn_conv_w, ffn_conv_b, w_ffn_down):
    b, s, _ = x.shape
    o1 = 3 * WIDTH_A
    o2 = o1 + 3 * WIDTH_B
    for l in range(DEPTH):
        hn = rms_norm(x, norm1_g[l])
        proj = hn @ w_in[l]
        qa, ka, va = [proj[..., i * WIDTH_A:(i + 1) * WIDTH_A].reshape(b, s, N_HEADS_A, HEAD_DIM)
                      for i in range(3)]
        qb, kb, vb = [proj[..., o1 + i * WIDTH_B:o1 + (i + 1) * WIDTH_B].reshape(b, s, N_HEADS_B, HEAD_DIM)
                      for i in range(3)]
        gate_a = proj[..., o2:o2 + D_MODEL]
        gate_b = proj[..., o2 + D_MODEL:o2 + 2 * D_MODEL]
        out_a = chunked_attention(qa, ka, va, q_norm_g[l], k_norm_g[l], rel_bias[l])
        out_b = stick_breaking_attention(qb, kb, vb)
        mixed = (jax.nn.sigmoid(gate_a) * (out_a @ w_branch_a[l])
                 + jax.nn.sigmoid(gate_b) * (out_b @ w_branch_b[l]))
        x = x + mixed @ w_out[l]
        x = x + conv_glu(rms_norm(x, norm2_g[l]), w_ffn_up[l], ffn_conv_w[l],
                         ffn_conv_b[l], w_ffn_down[l])
    return x
```

```python
import functools

import jax
import jax.numpy as jnp
import numpy as np
from jax import lax
from jax.experimental import pallas as pl
from jax.experimental.pallas import tpu as pltpu

D_MODEL = 1024
CHUNK = 64
LEFT_CHUNKS = 8
HEAD_DIM = 64
N_HEADS = 8
WIDTH = N_HEADS * HEAD_DIM
MAX_REL = 128
D_FF = 2816
EPS = 1e-6

LANES = 128
HEADS_PER_BLOCK = LANES // HEAD_DIM
N_PAIRS = N_HEADS // HEADS_PER_BLOCK
IN_WIDTH = 6 * WIDTH + 2 * D_MODEL

COL_QA, COL_KA, COL_VA = 0, WIDTH // LANES, 2 * WIDTH // LANES
COL_QB, COL_KB, COL_VB = 3 * WIDTH // LANES, 4 * WIDTH // LANES, 5 * WIDTH // LANES

ROW_TILE = 512
PROJ_CHUNK = 512
FFN_CHUNK = 256

QA_BLOCK = 128
BAND = LEFT_CHUNKS * CHUNK + QA_BLOCK
PAD = LEFT_CHUNKS * CHUNK
Q_TILE = 2048

QB_BLOCK = 128
KB_BLOCK = 128
DEAD_LOG = -104.0
NEG_BIG = -1e30

VMEM_LIMIT = 56 * 1024 * 1024

f32 = jnp.float32
bf16 = jnp.bfloat16


def _resident(shape, index_map):
    return pl.BlockSpec(shape, index_map, pipeline_mode=pl.Buffered(1))


def _nt_dot(a, b):
    return lax.dot_general(a, b, (((1,), (1,)), ((), ())), preferred_element_type=f32)


def _dot(a, b):
    return jnp.dot(a, b, preferred_element_type=f32)


def _sigmoid(x):
    return 1.0 / (1.0 + jnp.exp(-x))


def _in_proj_kernel(x_ref, g_ref, w_ref, seg_ref, qg_ref, kg_ref, o_ref):
    x = x_ref[...]
    ms = jnp.mean(x * x, axis=-1, keepdims=True)
    hn = (x * lax.rsqrt(ms + EPS) * g_ref[...]).astype(bf16)
    n_chunks = IN_WIDTH // PROJ_CHUNK
    gate_first = 6 * WIDTH // PROJ_CHUNK
    for c in range(n_chunks):
        cols = slice(c * PROJ_CHUNK, (c + 1) * PROJ_CHUNK)
        acc = _dot(hn, w_ref[:, cols])
        if c in (0, 1):
            ss = _dot((acc * acc).astype(bf16), seg_ref[...])
            gain = qg_ref[...] if c == 0 else kg_ref[...]
            acc = acc * lax.rsqrt(ss * (1.0 / HEAD_DIM) + EPS) * gain
        if c in (0, 3):
            acc = acc * (HEAD_DIM ** -0.5)
        if c >= gate_first:
            acc = _sigmoid(acc)
        o_ref[:, cols] = acc.astype(bf16)


def _in_proj(x2d, norm_g, w_in, seg, qg, kg):
    t = x2d.shape[0]
    return pl.pallas_call(
        _in_proj_kernel,
        grid=(t // ROW_TILE,),
        in_specs=[
            pl.BlockSpec((ROW_TILE, D_MODEL), lambda i: (i, 0)),
            _resident((1, D_MODEL), lambda i: (0, 0)),
            _resident((D_MODEL, IN_WIDTH), lambda i: (0, 0)),
            _resident((WIDTH, WIDTH), lambda i: (0, 0)),
            _resident((1, WIDTH), lambda i: (0, 0)),
            _resident((1, WIDTH), lambda i: (0, 0)),
        ],
        out_specs=pl.BlockSpec((ROW_TILE, IN_WIDTH), lambda i: (i, 0)),
        out_shape=jax.ShapeDtypeStruct((t, IN_WIDTH), bf16),
        compiler_params=pltpu.CompilerParams(
            dimension_semantics=("parallel",), vmem_limit_bytes=VMEM_LIMIT),
        name="in_proj",
    )(x2d, norm_g, w_in, seg, qg, kg)


def _attn_a_kernel(q_ref, k_ref, v_ref, bias_ref, o_ref, kpad, vpad):
    t = pl.program_id(2)

    @pl.when(t == 0)
    def _():
        zeros = jnp.zeros((PAD, LANES), bf16)
        kpad[0:PAD, :] = zeros
        vpad[0:PAD, :] = zeros
        kpad[PAD:, :] = k_ref[...]
        vpad[PAD:, :] = v_ref[...]

    lane = lax.broadcasted_iota(jnp.int32, (QA_BLOCK, LANES), 1)
    col = lax.broadcasted_iota(jnp.int32, (QA_BLOCK, BAND), 1)
    blocks_per_tile = Q_TILE // QA_BLOCK

    def body(ii, carry):
        qi = t * blocks_per_tile + ii
        r0 = pl.multiple_of(ii * QA_BLOCK, QA_BLOCK)
        b0 = pl.multiple_of(qi * QA_BLOCK, QA_BLOCK)
        q2 = q_ref[pl.ds(r0, QA_BLOCK), :]
        kb = kpad[pl.ds(b0, BAND), :]
        vb = vpad[pl.ds(b0, BAND), :]
        before_start = col < (PAD - qi * QA_BLOCK)
        outs = []
        for h in range(HEADS_PER_BLOCK):
            in_head = (lane >= h * HEAD_DIM) & (lane < (h + 1) * HEAD_DIM)
            qh = jnp.where(in_head, q2, jnp.zeros_like(q2))
            s = _nt_dot(qh, kb) + bias_ref[h]
            s = jnp.where(before_start, NEG_BIG, s)
            m = jnp.max(s, axis=-1, keepdims=True)
            p = jnp.exp(s - m)
            l = jnp.sum(p, axis=-1, keepdims=True)
            outs.append(_dot(p.astype(bf16), vb) / l)
        o = jnp.where(lane < HEAD_DIM, outs[0], outs[1])
        o_ref[pl.ds(r0, QA_BLOCK), :] = o.astype(bf16)
        return carry

    lax.fori_loop(0, blocks_per_tile, body, 0)


def _attn_a(proj3, bias_tab):
    b, s, _ = proj3.shape
    return pl.pallas_call(
        _attn_a_kernel,
        grid=(b, N_PAIRS, s // Q_TILE),
        in_specs=[
            pl.BlockSpec((None, Q_TILE, LANES), lambda bi, p, t: (bi, t, COL_QA + p)),
            pl.BlockSpec((None, s, LANES), lambda bi, p, t: (bi, 0, COL_KA + p)),
            pl.BlockSpec((None, s, LANES), lambda bi, p, t: (bi, 0, COL_VA + p)),
            pl.BlockSpec((HEADS_PER_BLOCK, QA_BLOCK, BAND), lambda bi, p, t: (p, 0, 0)),
        ],
        out_specs=pl.BlockSpec((None, Q_TILE, LANES), lambda bi, p, t: (bi, t, p)),
        out_shape=jax.ShapeDtypeStruct((b, s, WIDTH), bf16),
        scratch_shapes=[pltpu.VMEM((PAD + s, LANES), bf16),
                        pltpu.VMEM((PAD + s, LANES), bf16)],
        compiler_params=pltpu.CompilerParams(
            dimension_semantics=("parallel", "parallel", "arbitrary"),
            vmem_limit_bytes=VMEM_LIMIT),
        name="attn_a",
    )(proj3, proj3, proj3, bias_tab)


def _attn_b_kernel(q_ref, k_ref, v_ref, o_ref):
    t = pl.program_id(2)
    lane = lax.broadcasted_iota(jnp.int32, (QB_BLOCK, LANES), 1)
    row = lax.broadcasted_iota(jnp.int32, (QB_BLOCK, KB_BLOCK), 0)
    col = lax.broadcasted_iota(jnp.int32, (QB_BLOCK, KB_BLOCK), 1)
    causal = col < row
    tri = (lax.broadcasted_iota(jnp.int32, (KB_BLOCK, KB_BLOCK), 0)
           >= lax.broadcasted_iota(jnp.int32, (KB_BLOCK, KB_BLOCK), 1)).astype(bf16)
    blocks_per_tile = Q_TILE // QB_BLOCK

    def tile(qh, k0, carry, acc, diag):
        kt = k_ref[pl.ds(k0, KB_BLOCK), :]
        vt = v_ref[pl.ds(k0, KB_BLOCK), :]
        z = _nt_dot(qh, kt)
        log_keep = -(jnp.maximum(z, 0.0) + jnp.log1p(jnp.exp(-jnp.abs(z))))
        if diag:
            log_keep = jnp.where(causal, log_keep, 0.0)
        hi = log_keep.astype(bf16)
        lo = (log_keep - hi.astype(f32)).astype(bf16)
        incl = _dot(hi, tri) + _dot(lo, tri)
        w = jnp.exp(z + incl + carry)
        if diag:
            w = jnp.where(causal, w, 0.0)
        acc = acc + _dot(w.astype(bf16), vt)
        carry = carry + incl[:, 0:1]
        return carry, acc

    def sweep(qh, q0):
        zero_c = jnp.zeros((QB_BLOCK, 1), f32)
        zero_a = jnp.zeros((QB_BLOCK, LANES), f32)
        carry, acc = tile(qh, q0, zero_c, zero_a, True)

        def alive_of(c):
            return (jnp.max(c) > DEAD_LOG).astype(jnp.int32)

        def cond(st):
            j, _, _, alive = st
            return jnp.logical_and(j >= 0, alive > 0)

        def body(st):
            j, c, a, _ = st
            k0 = pl.multiple_of(j * KB_BLOCK, KB_BLOCK)
            c, a = tile(qh, k0, c, a, False)
            return j - 1, c, a, alive_of(c)

        j0 = q0 // KB_BLOCK - 1
        _, _, acc, _ = lax.while_loop(cond, body, (j0, carry, acc, alive_of(carry)))
        return acc

    def body(ii, carry):
        r0 = pl.multiple_of(ii * QB_BLOCK, QB_BLOCK)
        q0 = pl.multiple_of(t * Q_TILE + ii * QB_BLOCK, QB_BLOCK)
        q2 = q_ref[pl.ds(r0, QB_BLOCK), :]
        outs = []
        for h in range(HEADS_PER_BLOCK):
            in_head = (lane >= h * HEAD_DIM) & (lane < (h + 1) * HEAD_DIM)
            qh = jnp.where(in_head, q2, jnp.zeros_like(q2))
            outs.append(sweep(qh, q0))
        o = jnp.where(lane < HEAD_DIM, outs[0], outs[1])
        o_ref[pl.ds(r0, QB_BLOCK), :] = o.astype(bf16)
        return carry

    lax.fori_loop(0, blocks_per_tile, body, 0)


def _attn_b(proj3):
    b, s, _ = proj3.shape
    return pl.pallas_call(
        _attn_b_kernel,
        grid=(b, N_PAIRS, s // Q_TILE),
        in_specs=[
            pl.BlockSpec((None, Q_TILE, LANES), lambda bi, p, t: (bi, t, COL_QB + p)),
            pl.BlockSpec((None, s, LANES), lambda bi, p, t: (bi, 0, COL_KB + p)),
            pl.BlockSpec((None, s, LANES), lambda bi, p, t: (bi, 0, COL_VB + p)),
        ],
        out_specs=pl.BlockSpec((None, Q_TILE, LANES), lambda bi, p, t: (bi, t, p)),
        out_shape=jax.ShapeDtypeStruct((b, s, WIDTH), bf16),
        compiler_params=pltpu.CompilerParams(
            dimension_semantics=("parallel", "parallel", "arbitrary"),
            vmem_limit_bytes=VMEM_LIMIT),
        name="attn_b",
    )(proj3, proj3, proj3)


def _mix_kernel(x_ref, oa_ref, ob_ref, ga_ref, gb_ref, wa_ref, wb_ref, wo_ref, o_ref):
    ya = _dot(oa_ref[...], wa_ref[...])
    yb = _dot(ob_ref[...], wb_ref[...])
    mixed = ga_ref[...].astype(f32) * ya + gb_ref[...].astype(f32) * yb
    o_ref[...] = x_ref[...] + _dot(mixed.astype(bf16), wo_ref[...])


def _mix(x2d, oa, ob, proj2d, wa, wb, wo):
    t = x2d.shape[0]
    gate_a_blk = 6 * WIDTH // D_MODEL
    row_spec = lambda w: pl.BlockSpec((ROW_TILE, w), lambda i: (i, 0))
    return pl.pallas_call(
        _mix_kernel,
        grid=(t // ROW_TILE,),
        in_specs=[
            row_spec(D_MODEL), row_spec(WIDTH), row_spec(WIDTH),
            pl.BlockSpec((ROW_TILE, D_MODEL), lambda i: (i, gate_a_blk)),
            pl.BlockSpec((ROW_TILE, D_MODEL), lambda i: (i, gate_a_blk + 1)),
            _resident((WIDTH, D_MODEL), lambda i: (0, 0)),
            _resident((WIDTH, D_MODEL), lambda i: (0, 0)),
            _resident((D_MODEL, D_MODEL), lambda i: (0, 0)),
        ],
        out_specs=row_spec(D_MODEL),
        out_shape=jax.ShapeDtypeStruct((t, D_MODEL), f32),
        compiler_params=pltpu.CompilerParams(
            dimension_semantics=("parallel",), vmem_limit_bytes=VMEM_LIMIT),
        name="mix",
    )(x2d, oa, ob, proj2d, proj2d, wa, wb, wo)


def _ffn_kernel(x_ref, g_ref, wup_ref, cw_ref, cb_ref, wdn_ref, o_ref, tail_ref):
    @pl.when(pl.program_id(1) == 0)
    def _():
        tail_ref[...] = jnp.zeros_like(tail_ref)

    x = x_ref[...]
    ms = jnp.mean(x * x, axis=-1, keepdims=True)
    hn = (x * lax.rsqrt(ms + EPS) * g_ref[...]).astype(bf16)
    row = lax.broadcasted_iota(jnp.int32, (ROW_TILE, FFN_CHUNK), 0)
    o_ref[...] = x
    for c in range(D_FF // FFN_CHUNK):
        halves = []
        for half in range(2):
            c0 = half * D_FF + c * FFN_CHUNK
            cols = slice(c0, c0 + FFN_CHUNK)
            h = _dot(hn, wup_ref[:, cols])
            prev = tail_ref[:, cols]
            tail_ref[:, cols] = h[ROW_TILE - 8:, :]
            h1 = jnp.where(row == 0, prev[7:8, :], pltpu.roll(h, 1, 0))
            h2 = jnp.where(row == 0, prev[6:7, :],
                           jnp.where(row == 1, prev[7:8, :], pltpu.roll(h, 2, 0)))
            cw = cw_ref[:, cols]
            halves.append(cw[0:1, :] * h2 + cw[1:2, :] * h1 + cw[2:3, :] * h + cb_ref[:, cols])
        gate, up = halves
        act = (gate * _sigmoid(gate) * up).astype(bf16)
        o_ref[...] += _dot(act, wdn_ref[c * FFN_CHUNK:(c + 1) * FFN_CHUNK, :])


def _ffn(x3, norm_g, w_up, conv_w, conv_b, w_down):
    b, s, _ = x3.shape
    return pl.pallas_call(
        _ffn_kernel,
        grid=(b, s // ROW_TILE),
        in_specs=[
            pl.BlockSpec((None, ROW_TILE, D_MODEL), lambda bi, i: (bi, i, 0)),
            _resident((1, D_MODEL), lambda bi, i: (0, 0)),
            _resident((D_MODEL, 2 * D_FF), lambda bi, i: (0, 0)),
            _resident((3, 2 * D_FF), lambda bi, i: (0, 0)),
            _resident((1, 2 * D_FF), lambda bi, i: (0, 0)),
            _resident((D_FF, D_MODEL), lambda bi, i: (0, 0)),
        ],
        out_specs=pl.BlockSpec((None, ROW_TILE, D_MODEL), lambda bi, i: (bi, i, 0)),
        out_shape=jax.ShapeDtypeStruct((b, s, D_MODEL), f32),
        scratch_shapes=[pltpu.VMEM((8, 2 * D_FF), f32)],
        compiler_params=pltpu.CompilerParams(
            dimension_semantics=("parallel", "arbitrary"), vmem_limit_bytes=VMEM_LIMIT),
        name="ffn",
    )(x3, norm_g, w_up, conv_w, conv_b, w_down)


def _band_bias_table(rel_bias):
    r = np.arange(QA_BLOCK)[:, None]
    key_off = np.arange(BAND)[None, :] - PAD
    idx = np.clip(r - key_off, -MAX_REL, MAX_REL) + MAX_REL
    q_chunk = r // CHUNK
    k_chunk = np.floor_divide(key_off, CHUNK)
    in_band = (k_chunk <= q_chunk) & (k_chunk >= q_chunk - LEFT_CHUNKS)
    tab = rel_bias.astype(f32)[:, idx]
    return jnp.where(jnp.asarray(in_band)[None], tab, NEG_BIG)


def kernel(x, norm1_g, w_in, q_norm_g, k_norm_g, rel_bias, w_branch_a, w_branch_b, w_out,
           norm2_g, w_ffn_up, ffn_conv_w, ffn_conv_b, w_ffn_down):
    b, s, d = x.shape
    depth = w_in.shape[0]
    head_of = np.arange(WIDTH) // HEAD_DIM
    seg = jnp.asarray(head_of[:, None] == head_of[None, :], dtype=bf16)
    for l in range(depth):
        qg = jnp.tile(q_norm_g[l].astype(f32), N_HEADS)[None, :]
        kg = jnp.tile(k_norm_g[l].astype(f32), N_HEADS)[None, :]
        proj = _in_proj(x.reshape(b * s, d), norm1_g[l][None, :], w_in[l].astype(bf16), seg, qg, kg)
        proj3 = proj.reshape(b, s, IN_WIDTH)
        out_a = _attn_a(proj3, _band_bias_table(rel_bias[l]))
        out_b = _attn_b(proj3)
        x1 = _mix(x.reshape(b * s, d), out_a.reshape(b * s, WIDTH), out_b.reshape(b * s, WIDTH),
                  proj, w_branch_a[l].astype(bf16), w_branch_b[l].astype(bf16),
                  w_out[l].astype(bf16))
        x = _ffn(x1.reshape(b, s, d), norm2_g[l][None, :], w_ffn_up[l].astype(bf16),
                 ffn_conv_w[l], ffn_conv_b[l][None, :], w_ffn_down[l].astype(bf16))
    return x
```

```python
import math

import jax
import jax.numpy as jnp
import numpy as np
from jax import lax
from jax.experimental import pallas as pl
from jax.experimental.pallas import tpu as pltpu

D_MODEL = 1024
CHUNK = 64
LEFT_CHUNKS = 8
HEAD_DIM = 64
N_HEADS = 8
WIDTH = N_HEADS * HEAD_DIM
MAX_REL = 128
D_FF = 2816
CONV_WIDTH = 3
EPS = 1e-6

LANES = 128
SUBLANES = 8
HEADS_PER_BLOCK = LANES // HEAD_DIM
N_PAIRS = N_HEADS // HEADS_PER_BLOCK
IN_WIDTH = 6 * WIDTH + 2 * D_MODEL

COL_QA, COL_KA, COL_VA = 0, WIDTH // LANES, 2 * WIDTH // LANES
COL_QB, COL_KB, COL_VB = 3 * WIDTH // LANES, 4 * WIDTH // LANES, 5 * WIDTH // LANES

ROW_TILE = 512
PROJ_CHUNK = 512
FFN_CHUNK = 256

QA_BLOCK = 128
BAND = LEFT_CHUNKS * CHUNK + QA_BLOCK
PAD = LEFT_CHUNKS * CHUNK
Q_TILE = 2048

QB_BLOCK = 256
KB_BLOCK = QB_BLOCK
LOG2E = math.log2(math.e)
DEAD_LOG2 = -151.0
NEG_BIG = -1e30

VMEM_LIMIT = 56 * 1024 * 1024

f32 = jnp.float32
bf16 = jnp.bfloat16


def _resident(shape, index_map):
    return pl.BlockSpec(shape, index_map, pipeline_mode=pl.Buffered(1))


def _nt_dot(a, b):
    return lax.dot_general(a, b, (((1,), (1,)), ((), ())), preferred_element_type=f32)


def _dot(a, b):
    return jnp.dot(a, b, preferred_element_type=f32)


def _sigmoid(x):
    return 1.0 / (1.0 + jnp.exp(-x))


def _stack_heads(q2, lane):
    return jnp.concatenate(
        [jnp.where((lane >= h * HEAD_DIM) & (lane < (h + 1) * HEAD_DIM), q2, jnp.zeros_like(q2))
         for h in range(HEADS_PER_BLOCK)], axis=0)


def _in_proj_kernel(x_ref, g_ref, w_ref, seg_ref, qg_ref, kg_ref, o_ref):
    x = x_ref[...]
    ms = jnp.mean(x * x, axis=-1, keepdims=True)
    hn = (x * lax.rsqrt(ms + EPS) * g_ref[...]).astype(bf16)
    n_chunks = IN_WIDTH // PROJ_CHUNK
    gate_first = 6 * WIDTH // PROJ_CHUNK
    for c in range(n_chunks):
        cols = slice(c * PROJ_CHUNK, (c + 1) * PROJ_CHUNK)
        acc = _dot(hn, w_ref[:, cols])
        if c in (0, 1):
            ss = _dot((acc * acc).astype(bf16), seg_ref[...])
            gain = qg_ref[...] if c == 0 else kg_ref[...]
            acc = acc * lax.rsqrt(ss * (1.0 / HEAD_DIM) + EPS) * gain
        if c == 0:
            acc = acc * (HEAD_DIM ** -0.5)
        if c == 3:
            acc = acc * (HEAD_DIM ** -0.5 * LOG2E)
        if c >= gate_first:
            acc = _sigmoid(acc)
        o_ref[:, cols] = acc.astype(bf16)


def _in_proj(x2d, norm_g, w_in, seg, qg, kg):
    t = x2d.shape[0]
    return pl.pallas_call(
        _in_proj_kernel,
        grid=(t // ROW_TILE,),
        in_specs=[
            pl.BlockSpec((ROW_TILE, D_MODEL), lambda i: (i, 0)),
            _resident((1, D_MODEL), lambda i: (0, 0)),
            _resident((D_MODEL, IN_WIDTH), lambda i: (0, 0)),
            _resident((WIDTH, WIDTH), lambda i: (0, 0)),
            _resident((1, WIDTH), lambda i: (0, 0)),
            _resident((1, WIDTH), lambda i: (0, 0)),
        ],
        out_specs=pl.BlockSpec((ROW_TILE, IN_WIDTH), lambda i: (i, 0)),
        out_shape=jax.ShapeDtypeStruct((t, IN_WIDTH), bf16),
        compiler_params=pltpu.CompilerParams(
            dimension_semantics=("parallel",), vmem_limit_bytes=VMEM_LIMIT),
        name="in_proj",
    )(x2d, norm_g, w_in, seg, qg, kg)


def _attn_a_kernel(q_ref, k_ref, v_ref, bias_ref, o_ref, kpad, vpad):
    t = pl.program_id(2)

    @pl.when(t == 0)
    def _():
        zeros = jnp.zeros((PAD, LANES), bf16)
        kpad[0:PAD, :] = zeros
        vpad[0:PAD, :] = zeros
        kpad[PAD:, :] = k_ref[...]
        vpad[PAD:, :] = v_ref[...]

    rows = HEADS_PER_BLOCK * QA_BLOCK
    lane = lax.broadcasted_iota(jnp.int32, (QA_BLOCK, LANES), 1)
    col = lax.broadcasted_iota(jnp.int32, (rows, BAND), 1)
    blocks_per_tile = Q_TILE // QA_BLOCK

    def body(ii, carry):
        qi = t * blocks_per_tile + ii
        r0 = pl.multiple_of(ii * QA_BLOCK, QA_BLOCK)
        b0 = pl.multiple_of(qi * QA_BLOCK, QA_BLOCK)
        qs = _stack_heads(q_ref[pl.ds(r0, QA_BLOCK), :], lane)
        kb = kpad[pl.ds(b0, BAND), :]
        vb = vpad[pl.ds(b0, BAND), :]
        s = _nt_dot(qs, kb) + bias_ref[...].reshape(rows, BAND)
        s = jnp.where(col < (PAD - qi * QA_BLOCK), NEG_BIG, s)
        m = jnp.max(s, axis=-1, keepdims=True)
        p = jnp.exp(s - m)
        l = jnp.sum(p, axis=-1, keepdims=True)
        o = _dot(p.astype(bf16), vb) * (1.0 / l)
        o = jnp.where(lane < HEAD_DIM, o[:QA_BLOCK], o[QA_BLOCK:])
        o_ref[pl.ds(r0, QA_BLOCK), :] = o.astype(bf16)
        return carry

    lax.fori_loop(0, blocks_per_tile, body, 0, unroll=2)


def _attn_a(proj3, bias_tab):
    b, s, _ = proj3.shape
    return pl.pallas_call(
        _attn_a_kernel,
        grid=(b, N_PAIRS, s // Q_TILE),
        in_specs=[
            pl.BlockSpec((None, Q_TILE, LANES), lambda bi, p, t: (bi, t, COL_QA + p)),
            pl.BlockSpec((None, s, LANES), lambda bi, p, t: (bi, 0, COL_KA + p)),
            pl.BlockSpec((None, s, LANES), lambda bi, p, t: (bi, 0, COL_VA + p)),
            pl.BlockSpec((HEADS_PER_BLOCK, QA_BLOCK, BAND), lambda bi, p, t: (p, 0, 0)),
        ],
        out_specs=pl.BlockSpec((None, Q_TILE, LANES), lambda bi, p, t: (bi, t, p)),
        out_shape=jax.ShapeDtypeStruct((b, s, WIDTH), bf16),
        scratch_shapes=[pltpu.VMEM((PAD + s, LANES), bf16),
                        pltpu.VMEM((PAD + s, LANES), bf16)],
        compiler_params=pltpu.CompilerParams(
            dimension_semantics=("parallel", "parallel", "arbitrary"),
            vmem_limit_bytes=VMEM_LIMIT),
        name="attn_a",
    )(proj3, proj3, proj3, bias_tab)


def _neg_abs(x):
    bits = lax.bitcast_convert_type(x, jnp.uint32) | jnp.uint32(0x80000000)
    return lax.bitcast_convert_type(bits, f32)


def _attn_b_kernel(q_ref, k_ref, v_ref, o_ref):
    t = pl.program_id(2)
    rows = HEADS_PER_BLOCK * QB_BLOCK
    lane = lax.broadcasted_iota(jnp.int32, (QB_BLOCK, LANES), 1)
    row = lax.broadcasted_iota(jnp.int32, (rows, KB_BLOCK), 0)
    col = lax.broadcasted_iota(jnp.int32, (rows, KB_BLOCK), 1)
    causal = col < (row & (QB_BLOCK - 1))
    neg_tri = jnp.where(lax.broadcasted_iota(jnp.int32, (KB_BLOCK, KB_BLOCK), 0)
                        >= lax.broadcasted_iota(jnp.int32, (KB_BLOCK, KB_BLOCK), 1),
                        -1.0, 0.0).astype(bf16)
    blocks_per_tile = Q_TILE // QB_BLOCK

    def scores(qs, k0):
        z = _nt_dot(qs, k_ref[pl.ds(k0, KB_BLOCK), :])
        sp = jnp.maximum(z, 0.0) + jnp.log2(1.0 + jnp.exp2(_neg_abs(z)))
        return z, sp

    def weights(z, sp, diag):
        hi = sp.astype(bf16)
        lo = (sp - hi.astype(f32)).astype(bf16)
        log_keep_from = _dot(hi, neg_tri) + _dot(lo, neg_tri)
        w = jnp.exp2(z + log_keep_from)
        if diag:
            w = jnp.where(causal, w, 0.0)
        return w.astype(bf16), log_keep_from[:, 0:1]

    def alive_of(c):
        return (jnp.max(c) > DEAD_LOG2).astype(jnp.int32)

    def body(ii, loop_carry):
        r0 = pl.multiple_of(ii * QB_BLOCK, QB_BLOCK)
        q0 = pl.multiple_of(t * Q_TILE + ii * QB_BLOCK, QB_BLOCK)
        qs = _stack_heads(q_ref[pl.ds(r0, QB_BLOCK), :], lane)

        z_d, sp_d = scores(qs, q0)
        w_d, tot_d = weights(z_d, jnp.where(causal, sp_d, 0.0), True)
        has_prev = q0 > 0
        k_prev = pl.multiple_of(jnp.maximum(q0 - KB_BLOCK, 0), KB_BLOCK)
        z_p, sp_p = scores(jnp.where(has_prev, qs, jnp.zeros_like(qs)), k_prev)
        w_p, tot_p = weights(z_p, sp_p, False)
        scale_p = jnp.where(has_prev, jnp.exp2(tot_d), 0.0)
        acc = (_dot(w_d, v_ref[pl.ds(q0, KB_BLOCK), :])
               + _dot(w_p, v_ref[pl.ds(k_prev, KB_BLOCK), :]) * scale_p)
        carry = tot_d + tot_p

        def cond(st):
            j, _, _, alive = st
            return jnp.logical_and(j >= 0, alive > 0)

        def sweep(st):
            j, c, a, _ = st
            k0 = pl.multiple_of(j * KB_BLOCK, KB_BLOCK)
            z, sp = scores(qs, k0)
            w, tot = weights(z, sp, False)
            a = a + _dot(w, v_ref[pl.ds(k0, KB_BLOCK), :]) * jnp.exp2(c)
            c = c + tot
            return j - 1, c, a, alive_of(c)

        _, _, acc, _ = lax.while_loop(
            cond, sweep, (q0 // KB_BLOCK - 2, carry, acc, alive_of(carry)))
        o = jnp.where(lane < HEAD_DIM, acc[:QB_BLOCK], acc[QB_BLOCK:])
        o_ref[pl.ds(r0, QB_BLOCK), :] = o.astype(bf16)
        return loop_carry

    lax.fori_loop(0, blocks_per_tile, body, 0)


def _attn_b(proj3):
    b, s, _ = proj3.shape
    return pl.pallas_call(
        _attn_b_kernel,
        grid=(b, N_PAIRS, s // Q_TILE),
        in_specs=[
            pl.BlockSpec((None, Q_TILE, LANES), lambda bi, p, t: (bi, t, COL_QB + p)),
            pl.BlockSpec((None, s, LANES), lambda bi, p, t: (bi, 0, COL_KB + p)),
            pl.BlockSpec((None, s, LANES), lambda bi, p, t: (bi, 0, COL_VB + p)),
        ],
        out_specs=pl.BlockSpec((None, Q_TILE, LANES), lambda bi, p, t: (bi, t, p)),
        out_shape=jax.ShapeDtypeStruct((b, s, WIDTH), bf16),
        compiler_params=pltpu.CompilerParams(
            dimension_semantics=("parallel", "parallel", "arbitrary"),
            vmem_limit_bytes=VMEM_LIMIT),
        name="attn_b",
    )(proj3, proj3, proj3)


def _mix_kernel(x_ref, oa_ref, ob_ref, ga_ref, gb_ref, wa_ref, wb_ref, wo_ref, o_ref):
    ya = _dot(oa_ref[...], wa_ref[...])
    yb = _dot(ob_ref[...], wb_ref[...])
    mixed = ga_ref[...].astype(f32) * ya + gb_ref[...].astype(f32) * yb
    o_ref[...] = x_ref[...] + _dot(mixed.astype(bf16), wo_ref[...])


def _mix(x2d, oa, ob, proj2d, wa, wb, wo):
    t = x2d.shape[0]
    gate_a_blk = 6 * WIDTH // D_MODEL
    row_spec = lambda w: pl.BlockSpec((ROW_TILE, w), lambda i: (i, 0))
    return pl.pallas_call(
        _mix_kernel,
        grid=(t // ROW_TILE,),
        in_specs=[
            row_spec(D_MODEL), row_spec(WIDTH), row_spec(WIDTH),
            pl.BlockSpec((ROW_TILE, D_MODEL), lambda i: (i, gate_a_blk)),
            pl.BlockSpec((ROW_TILE, D_MODEL), lambda i: (i, gate_a_blk + 1)),
            _resident((WIDTH, D_MODEL), lambda i: (0, 0)),
            _resident((WIDTH, D_MODEL), lambda i: (0, 0)),
            _resident((D_MODEL, D_MODEL), lambda i: (0, 0)),
        ],
        out_specs=row_spec(D_MODEL),
        out_shape=jax.ShapeDtypeStruct((t, D_MODEL), f32),
        compiler_params=pltpu.CompilerParams(
            dimension_semantics=("parallel",), vmem_limit_bytes=VMEM_LIMIT),
        name="mix",
    )(x2d, oa, ob, proj2d, proj2d, wa, wb, wo)


def _ffn_kernel(x_ref, g_ref, wup_ref, cw_ref, cb_ref, wdn_ref, o_ref, tail_ref, hbuf, act_ref):
    @pl.when(pl.program_id(1) == 0)
    def _():
        tail_ref[...] = jnp.zeros_like(tail_ref)

    x = x_ref[...]
    ms = jnp.mean(x * x, axis=-1, keepdims=True)
    hn = (x * lax.rsqrt(ms + EPS) * g_ref[...]).astype(bf16)
    for c in range(D_FF // FFN_CHUNK):
        halves = []
        for half in range(2):
            c0 = half * D_FF + c * FFN_CHUNK
            cols = slice(c0, c0 + FFN_CHUNK)
            h = _dot(hn, wup_ref[:, cols])
            hbuf[half, 0:SUBLANES, :] = tail_ref[:, cols]
            hbuf[half, SUBLANES:, :] = h
            tail_ref[:, cols] = h[ROW_TILE - SUBLANES:, :]
            y = cb_ref[:, cols] + cw_ref[CONV_WIDTH - 1:CONV_WIDTH, cols] * h
            for back in range(1, CONV_WIDTH):
                shifted = hbuf[half, SUBLANES - back:SUBLANES - back + ROW_TILE, :]
                y = y + cw_ref[CONV_WIDTH - 1 - back:CONV_WIDTH - back, cols] * shifted
            halves.append(y)
        gate, up = halves
        act_ref[:, c * FFN_CHUNK:(c + 1) * FFN_CHUNK] = (gate * _sigmoid(gate) * up).astype(bf16)
    o_ref[...] = x + _dot(act_ref[...], wdn_ref[...])


def _ffn(x3, norm_g, w_up, conv_w, conv_b, w_down):
    b, s, _ = x3.shape
    return pl.pallas_call(
        _ffn_kernel,
        grid=(b, s // ROW_TILE),
        in_specs=[
            pl.BlockSpec((None, ROW_TILE, D_MODEL), lambda bi, i: (bi, i, 0)),
            _resident((1, D_MODEL), lambda bi, i: (0, 0)),
            _resident((D_MODEL, 2 * D_FF), lambda bi, i: (0, 0)),
            _resident((CONV_WIDTH, 2 * D_FF), lambda bi, i: (0, 0)),
            _resident((1, 2 * D_FF), lambda bi, i: (0, 0)),
            _resident((D_FF, D_MODEL), lambda bi, i: (0, 0)),
        ],
        out_specs=pl.BlockSpec((None, ROW_TILE, D_MODEL), lambda bi, i: (bi, i, 0)),
        out_shape=jax.ShapeDtypeStruct((b, s, D_MODEL), f32),
        scratch_shapes=[pltpu.VMEM((SUBLANES, 2 * D_FF), f32),
                        pltpu.VMEM((2, SUBLANES + ROW_TILE, FFN_CHUNK), f32),
                        pltpu.VMEM((ROW_TILE, D_FF), bf16)],
        compiler_params=pltpu.CompilerParams(
            dimension_semantics=("parallel", "arbitrary"), vmem_limit_bytes=VMEM_LIMIT),
        name="ffn",
    )(x3, norm_g, w_up, conv_w, conv_b, w_down)


def _band_bias_table(rel_bias):
    h = rel_bias.shape[0]
    period = BAND + QA_BLOCK
    far = rel_bias[:, 2 * MAX_REL:]
    by_offset = jnp.concatenate([
        jnp.broadcast_to(far, (h, PAD - MAX_REL + 1)),
        rel_bias[:, 2 * MAX_REL - 1:0:-1],
        jnp.zeros((h, 1), rel_bias.dtype),
        jnp.broadcast_to(far, (h, QA_BLOCK - 1)),
    ], axis=1).astype(f32)
    tab = jnp.tile(by_offset, (1, QA_BLOCK))[:, :QA_BLOCK * (period - 1)]
    tab = tab.reshape(h, QA_BLOCK, period - 1)[:, :, :BAND]
    q_chunk = np.arange(QA_BLOCK)[:, None] // CHUNK
    k_chunk = np.floor_divide(np.arange(BAND)[None, :] - PAD, CHUNK)
    in_band = (k_chunk <= q_chunk) & (k_chunk >= q_chunk - LEFT_CHUNKS)
    return jnp.where(jnp.asarray(in_band)[None], tab, NEG_BIG)


def kernel(x, norm1_g, w_in, q_norm_g, k_norm_g, rel_bias, w_branch_a, w_branch_b, w_out,
           norm2_g, w_ffn_up, ffn_conv_w, ffn_conv_b, w_ffn_down):
    b, s, d = x.shape
    depth = w_in.shape[0]
    head_of = np.arange(WIDTH) // HEAD_DIM
    seg = jnp.asarray(head_of[:, None] == head_of[None, :], dtype=bf16)
    for l in range(depth):
        qg = jnp.tile(q_norm_g[l].astype(f32), N_HEADS)[None, :]
        kg = jnp.tile(k_norm_g[l].astype(f32), N_HEADS)[None, :]
        proj = _in_proj(x.reshape(b * s, d), norm1_g[l][None, :], w_in[l].astype(bf16), seg, qg, kg)
        proj3 = proj.reshape(b, s, IN_WIDTH)
        out_a = _attn_a(proj3, _band_bias_table(rel_bias[l]))
        out_b = _attn_b(proj3)
        x1 = _mix(x.reshape(b * s, d), out_a.reshape(b * s, WIDTH), out_b.reshape(b * s, WIDTH),
                  proj, w_branch_a[l].astype(bf16), w_branch_b[l].astype(bf16),
                  w_out[l].astype(bf16))
        x = _ffn(x1.reshape(b, s, d), norm2_g[l][None, :], w_ffn_up[l].astype(bf16),
                 ffn_conv_w[l], ffn_conv_b[l][None, :], w_ffn_down[l].astype(bf16))
    return x
```

```python
import math

import jax
import jax.numpy as jnp
import numpy as np
from jax import lax
from jax.experimental import pallas as pl
from jax.experimental.pallas import tpu as pltpu

D_MODEL = 1024
CHUNK = 64
LEFT_CHUNKS = 8
HEAD_DIM = 64
N_HEADS = 8
WIDTH = N_HEADS * HEAD_DIM
MAX_REL = 128
D_FF = 2816
CONV_WIDTH = 3
EPS = 1e-6

LANES = 128
SUBLANES = 8
HEADS_PER_BLOCK = LANES // HEAD_DIM
N_PAIRS = N_HEADS // HEADS_PER_BLOCK
IN_WIDTH = 6 * WIDTH + 2 * D_MODEL

COL_QA, COL_KA, COL_VA = 0, WIDTH // LANES, 2 * WIDTH // LANES
COL_QB, COL_KB, COL_VB = 3 * WIDTH // LANES, 4 * WIDTH // LANES, 5 * WIDTH // LANES

ROW_TILE = 512
PROJ_CHUNK = 512
FFN_CHUNK = 256

QA_BLOCK = 128
BAND = LEFT_CHUNKS * CHUNK + QA_BLOCK
PAD = LEFT_CHUNKS * CHUNK
Q_TILE = 2048
A_UNROLL = 4

QB_BLOCK = 256
KB_BLOCK = QB_BLOCK
LOG2E = math.log2(math.e)
DEAD_LOG2 = -151.0
NEG_BIG = -1e30

VMEM_LIMIT = 56 * 1024 * 1024

f32 = jnp.float32
bf16 = jnp.bfloat16


def _resident(shape, index_map):
    return pl.BlockSpec(shape, index_map, pipeline_mode=pl.Buffered(1))


def _nt_dot(a, b):
    return lax.dot_general(a, b, (((1,), (1,)), ((), ())), preferred_element_type=f32)


def _dot(a, b):
    return jnp.dot(a, b, preferred_element_type=f32)


def _sigmoid(x):
    return 1.0 / (1.0 + jnp.exp(-x))


def _stack_heads(q2, lane):
    return jnp.concatenate(
        [jnp.where((lane >= h * HEAD_DIM) & (lane < (h + 1) * HEAD_DIM), q2, jnp.zeros_like(q2))
         for h in range(HEADS_PER_BLOCK)], axis=0)


def _in_proj_kernel(x_ref, g_ref, w_ref, seg_ref, qg_ref, kg_ref, o_ref):
    x = x_ref[...]
    ms = jnp.mean(x * x, axis=-1, keepdims=True)
    hn = (x * lax.rsqrt(ms + EPS) * g_ref[...]).astype(bf16)
    n_chunks = IN_WIDTH // PROJ_CHUNK
    gate_first = 6 * WIDTH // PROJ_CHUNK
    for c in range(n_chunks):
        cols = slice(c * PROJ_CHUNK, (c + 1) * PROJ_CHUNK)
        acc = _dot(hn, w_ref[:, cols])
        if c in (0, 1):
            ss = _dot((acc * acc).astype(bf16), seg_ref[...])
            gain = qg_ref[...] if c == 0 else kg_ref[...]
            acc = acc * lax.rsqrt(ss * (1.0 / HEAD_DIM) + EPS) * gain
        if c in (0, 3):
            acc = acc * (HEAD_DIM ** -0.5 * LOG2E)
        if c >= gate_first:
            acc = _sigmoid(acc)
        o_ref[:, cols] = acc.astype(bf16)


def _in_proj(x2d, norm_g, w_in, seg, qg, kg):
    t = x2d.shape[0]
    return pl.pallas_call(
        _in_proj_kernel,
        grid=(t // ROW_TILE,),
        in_specs=[
            pl.BlockSpec((ROW_TILE, D_MODEL), lambda i: (i, 0)),
            _resident((1, D_MODEL), lambda i: (0, 0)),
            _resident((D_MODEL, IN_WIDTH), lambda i: (0, 0)),
            _resident((WIDTH, WIDTH), lambda i: (0, 0)),
            _resident((1, WIDTH), lambda i: (0, 0)),
            _resident((1, WIDTH), lambda i: (0, 0)),
        ],
        out_specs=pl.BlockSpec((ROW_TILE, IN_WIDTH), lambda i: (i, 0)),
        out_shape=jax.ShapeDtypeStruct((t, IN_WIDTH), bf16),
        compiler_params=pltpu.CompilerParams(
            dimension_semantics=("parallel",), vmem_limit_bytes=VMEM_LIMIT),
        name="in_proj",
    )(x2d, norm_g, w_in, seg, qg, kg)


def _attn_a_kernel(q_ref, k_ref, v_ref, bias_ref, o_ref, kpad, vpad):
    t = pl.program_id(2)

    @pl.when(t == 0)
    def _():
        zeros = jnp.zeros((PAD, LANES), bf16)
        kpad[0:PAD, :] = zeros
        vpad[0:PAD, :] = zeros
        kpad[PAD:, :] = k_ref[...]
        vpad[PAD:, :] = v_ref[...]

    rows = HEADS_PER_BLOCK * QA_BLOCK
    lane = lax.broadcasted_iota(jnp.int32, (QA_BLOCK, LANES), 1)
    col = lax.broadcasted_iota(jnp.int32, (rows, BAND), 1)
    blocks_per_tile = Q_TILE // QA_BLOCK

    def make_body(band_has_padding):
        def body(ii, carry):
            qi = t * blocks_per_tile + ii
            r0 = pl.multiple_of(ii * QA_BLOCK, QA_BLOCK)
            b0 = pl.multiple_of(qi * QA_BLOCK, QA_BLOCK)
            qs = _stack_heads(q_ref[pl.ds(r0, QA_BLOCK), :], lane)
            kb = kpad[pl.ds(b0, BAND), :]
            vb = vpad[pl.ds(b0, BAND), :]
            s = _nt_dot(qs, kb) + bias_ref[...].reshape(rows, BAND)
            if band_has_padding:
                s = jnp.where(col < (PAD - qi * QA_BLOCK), NEG_BIG, s)
            m = jnp.max(s, axis=-1, keepdims=True)
            p = jnp.exp2(s - m)
            l = jnp.sum(p, axis=-1, keepdims=True)
            o = _dot(p.astype(bf16), vb) * (1.0 / l)
            o = jnp.where(lane < HEAD_DIM, o[:QA_BLOCK], o[QA_BLOCK:])
            o_ref[pl.ds(r0, QA_BLOCK), :] = o.astype(bf16)
            return carry
        return body

    padded_blocks = PAD // QA_BLOCK

    @pl.when(t == 0)
    def _():
        lax.fori_loop(0, padded_blocks, make_body(True), 0, unroll=A_UNROLL)
        lax.fori_loop(padded_blocks, blocks_per_tile, make_body(False), 0, unroll=A_UNROLL)

    @pl.when(t != 0)
    def _():
        lax.fori_loop(0, blocks_per_tile, make_body(False), 0, unroll=A_UNROLL)


def _attn_a(proj3, bias_tab):
    b, s, _ = proj3.shape
    return pl.pallas_call(
        _attn_a_kernel,
        grid=(b, N_PAIRS, s // Q_TILE),
        in_specs=[
            pl.BlockSpec((None, Q_TILE, LANES), lambda bi, p, t: (bi, t, COL_QA + p)),
            pl.BlockSpec((None, s, LANES), lambda bi, p, t: (bi, 0, COL_KA + p)),
            pl.BlockSpec((None, s, LANES), lambda bi, p, t: (bi, 0, COL_VA + p)),
            pl.BlockSpec((HEADS_PER_BLOCK, QA_BLOCK, BAND), lambda bi, p, t: (p, 0, 0)),
        ],
        out_specs=pl.BlockSpec((None, Q_TILE, LANES), lambda bi, p, t: (bi, t, p)),
        out_shape=jax.ShapeDtypeStruct((b, s, WIDTH), bf16),
        scratch_shapes=[pltpu.VMEM((PAD + s, LANES), bf16),
                        pltpu.VMEM((PAD + s, LANES), bf16)],
        compiler_params=pltpu.CompilerParams(
            dimension_semantics=("parallel", "parallel", "arbitrary"),
            vmem_limit_bytes=VMEM_LIMIT),
        name="attn_a",
    )(proj3, proj3, proj3, bias_tab)


def _neg_abs(x):
    bits = lax.bitcast_convert_type(x, jnp.uint32) | jnp.uint32(0x80000000)
    return lax.bitcast_convert_type(bits, f32)


def _attn_b_kernel(q_ref, k_ref, v_ref, o_ref, acc_ref, carry_ref):
    t = pl.program_id(2)
    rows = HEADS_PER_BLOCK * QB_BLOCK
    lane = lax.broadcasted_iota(jnp.int32, (QB_BLOCK, LANES), 1)
    row = lax.broadcasted_iota(jnp.int32, (rows, KB_BLOCK), 0)
    col = lax.broadcasted_iota(jnp.int32, (rows, KB_BLOCK), 1)
    causal = col < (row & (QB_BLOCK - 1))
    neg_tri = jnp.where(lax.broadcasted_iota(jnp.int32, (KB_BLOCK, KB_BLOCK), 0)
                        >= lax.broadcasted_iota(jnp.int32, (KB_BLOCK, KB_BLOCK), 1),
                        -1.0, 0.0).astype(bf16)
    blocks_per_tile = Q_TILE // QB_BLOCK

    def scores(qs, k0):
        z = _nt_dot(qs, k_ref[pl.ds(k0, KB_BLOCK), :])
        sp = jnp.maximum(z, 0.0) + jnp.log2(1.0 + jnp.exp2(_neg_abs(z)))
        return z, sp

    def weights(z, sp, diag):
        log_keep_from = _dot(sp.astype(bf16), neg_tri)
        w = jnp.exp2(z + log_keep_from)
        if diag:
            w = jnp.where(causal, w, 0.0)
        return w.astype(bf16), log_keep_from[:, 0:1]

    def alive_of(c):
        return (jnp.max(c) > DEAD_LOG2).astype(jnp.int32)

    def load_q(ii):
        r0 = pl.multiple_of(ii * QB_BLOCK, QB_BLOCK)
        q0 = pl.multiple_of(t * Q_TILE + ii * QB_BLOCK, QB_BLOCK)
        return q0, _stack_heads(q_ref[pl.ds(r0, QB_BLOCK), :], lane)

    def emit(ii, acc):
        r0 = pl.multiple_of(ii * QB_BLOCK, QB_BLOCK)
        o = jnp.where(lane < HEAD_DIM, acc[:QB_BLOCK], acc[QB_BLOCK:])
        o_ref[pl.ds(r0, QB_BLOCK), :] = o.astype(bf16)

    def first_pass(ii, loop_carry):
        q0, qs = load_q(ii)
        z_d, sp_d = scores(qs, q0)
        w_d, tot_d = weights(z_d, jnp.where(causal, sp_d, 0.0), True)
        has_prev = q0 > 0
        k_prev = pl.multiple_of(jnp.maximum(q0 - KB_BLOCK, 0), KB_BLOCK)
        z_p, sp_p = scores(jnp.where(has_prev, qs, jnp.zeros_like(qs)), k_prev)
        w_p, tot_p = weights(z_p, sp_p, False)
        scale_p = jnp.where(has_prev, jnp.exp2(tot_d), 0.0)
        acc = (_dot(w_d, v_ref[pl.ds(q0, KB_BLOCK), :])
               + _dot(w_p, v_ref[pl.ds(k_prev, KB_BLOCK), :]) * scale_p)
        acc_ref[ii] = acc
        carry_ref[ii] = tot_d + tot_p
        emit(ii, acc)
        return loop_carry

    lax.fori_loop(0, blocks_per_tile, first_pass, 0, unroll=2)

    def older_blocks(ii, loop_carry):
        q0, qs = load_q(ii)

        def cond(st):
            j, _, _, alive = st
            return jnp.logical_and(j >= 0, alive > 0)

        def sweep(st):
            j, c, a, _ = st
            k0 = pl.multiple_of(j * KB_BLOCK, KB_BLOCK)
            z, sp = scores(qs, k0)
            w, tot = weights(z, sp, False)
            a = a + _dot(w, v_ref[pl.ds(k0, KB_BLOCK), :]) * jnp.exp2(c)
            c = c + tot
            return j - 1, c, a, alive_of(c)

        carry = carry_ref[ii]
        _, _, acc, _ = lax.while_loop(
            cond, sweep, (q0 // KB_BLOCK - 2, carry, acc_ref[ii], alive_of(carry)))
        emit(ii, acc)
        return loop_carry

    @pl.when(jnp.max(carry_ref[...]) > DEAD_LOG2)
    def _():
        lax.fori_loop(0, blocks_per_tile, older_blocks, 0)


def _attn_b(proj3):
    b, s, _ = proj3.shape
    return pl.pallas_call(
        _attn_b_kernel,
        grid=(b, N_PAIRS, s // Q_TILE),
        in_specs=[
            pl.BlockSpec((None, Q_TILE, LANES), lambda bi, p, t: (bi, t, COL_QB + p)),
            pl.BlockSpec((None, s, LANES), lambda bi, p, t: (bi, 0, COL_KB + p)),
            pl.BlockSpec((None, s, LANES), lambda bi, p, t: (bi, 0, COL_VB + p)),
        ],
        out_specs=pl.BlockSpec((None, Q_TILE, LANES), lambda bi, p, t: (bi, t, p)),
        out_shape=jax.ShapeDtypeStruct((b, s, WIDTH), bf16),
        scratch_shapes=[
            pltpu.VMEM((Q_TILE // QB_BLOCK, HEADS_PER_BLOCK * QB_BLOCK, LANES), f32),
            pltpu.VMEM((Q_TILE // QB_BLOCK, HEADS_PER_BLOCK * QB_BLOCK, 1), f32)],
        compiler_params=pltpu.CompilerParams(
            dimension_semantics=("parallel", "parallel", "arbitrary"),
            vmem_limit_bytes=VMEM_LIMIT),
        name="attn_b",
    )(proj3, proj3, proj3)


def _mix_kernel(x_ref, oa_ref, ob_ref, ga_ref, gb_ref, wa_ref, wb_ref, wo_ref, o_ref):
    ya = _dot(oa_ref[...], wa_ref[...])
    yb = _dot(ob_ref[...], wb_ref[...])
    mixed = ga_ref[...].astype(f32) * ya + gb_ref[...].astype(f32) * yb
    o_ref[...] = x_ref[...] + _dot(mixed.astype(bf16), wo_ref[...])


def _mix(x2d, oa, ob, proj2d, wa, wb, wo):
    t = x2d.shape[0]
    gate_a_blk = 6 * WIDTH // D_MODEL
    row_spec = lambda w: pl.BlockSpec((ROW_TILE, w), lambda i: (i, 0))
    return pl.pallas_call(
        _mix_kernel,
        grid=(t // ROW_TILE,),
        in_specs=[
            row_spec(D_MODEL), row_spec(WIDTH), row_spec(WIDTH),
            pl.BlockSpec((ROW_TILE, D_MODEL), lambda i: (i, gate_a_blk)),
            pl.BlockSpec((ROW_TILE, D_MODEL), lambda i: (i, gate_a_blk + 1)),
            _resident((WIDTH, D_MODEL), lambda i: (0, 0)),
            _resident((WIDTH, D_MODEL), lambda i: (0, 0)),
            _resident((D_MODEL, D_MODEL), lambda i: (0, 0)),
        ],
        out_specs=row_spec(D_MODEL),
        out_shape=jax.ShapeDtypeStruct((t, D_MODEL), f32),
        compiler_params=pltpu.CompilerParams(
            dimension_semantics=("parallel",), vmem_limit_bytes=VMEM_LIMIT),
        name="mix",
    )(x2d, oa, ob, proj2d, proj2d, wa, wb, wo)


def _ffn_kernel(x_ref, g_ref, wup_ref, cw_ref, cb_ref, wdn_ref, o_ref, tail_ref, hbuf, act_ref):
    @pl.when(pl.program_id(1) == 0)
    def _():
        tail_ref[...] = jnp.zeros_like(tail_ref)

    x = x_ref[...]
    ms = jnp.mean(x * x, axis=-1, keepdims=True)
    hn = (x * lax.rsqrt(ms + EPS) * g_ref[...]).astype(bf16)
    for c in range(D_FF // FFN_CHUNK):
        halves = []
        for half in range(2):
            c0 = half * D_FF + c * FFN_CHUNK
            cols = slice(c0, c0 + FFN_CHUNK)
            h = _dot(hn, wup_ref[:, cols])
            hbuf[half, 0:SUBLANES, :] = tail_ref[:, cols]
            hbuf[half, SUBLANES:, :] = h
            tail_ref[:, cols] = h[ROW_TILE - SUBLANES:, :]
            y = cb_ref[:, cols] + cw_ref[CONV_WIDTH - 1:CONV_WIDTH, cols] * h
            for back in range(1, CONV_WIDTH):
                shifted = hbuf[half, SUBLANES - back:SUBLANES - back + ROW_TILE, :]
                y = y + cw_ref[CONV_WIDTH - 1 - back:CONV_WIDTH - back, cols] * shifted
            halves.append(y)
        gate, up = halves
        act_ref[:, c * FFN_CHUNK:(c + 1) * FFN_CHUNK] = (gate * _sigmoid(gate) * up).astype(bf16)
    o_ref[...] = x + _dot(act_ref[...], wdn_ref[...])


def _ffn(x3, norm_g, w_up, conv_w, conv_b, w_down):
    b, s, _ = x3.shape
    return pl.pallas_call(
        _ffn_kernel,
        grid=(b, s // ROW_TILE),
        in_specs=[
            pl.BlockSpec((None, ROW_TILE, D_MODEL), lambda bi, i: (bi, i, 0)),
            _resident((1, D_MODEL), lambda bi, i: (0, 0)),
            _resident((D_MODEL, 2 * D_FF), lambda bi, i: (0, 0)),
            _resident((CONV_WIDTH, 2 * D_FF), lambda bi, i: (0, 0)),
            _resident((1, 2 * D_FF), lambda bi, i: (0, 0)),
            _resident((D_FF, D_MODEL), lambda bi, i: (0, 0)),
        ],
        out_specs=pl.BlockSpec((None, ROW_TILE, D_MODEL), lambda bi, i: (bi, i, 0)),
        out_shape=jax.ShapeDtypeStruct((b, s, D_MODEL), f32),
        scratch_shapes=[pltpu.VMEM((SUBLANES, 2 * D_FF), f32),
                        pltpu.VMEM((2, SUBLANES + ROW_TILE, FFN_CHUNK), f32),
                        pltpu.VMEM((ROW_TILE, D_FF), bf16)],
        compiler_params=pltpu.CompilerParams(
            dimension_semantics=("parallel", "arbitrary"), vmem_limit_bytes=VMEM_LIMIT),
        name="ffn",
    )(x3, norm_g, w_up, conv_w, conv_b, w_down)


def _band_bias_table(rel_bias):
    h = rel_bias.shape[0]
    period = BAND + QA_BLOCK
    far = rel_bias[:, 2 * MAX_REL:]
    by_offset = jnp.concatenate([
        jnp.broadcast_to(far, (h, PAD - MAX_REL + 1)),
        rel_bias[:, 2 * MAX_REL - 1:0:-1],
        jnp.zeros((h, 1), rel_bias.dtype),
        jnp.broadcast_to(far, (h, QA_BLOCK - 1)),
    ], axis=1).astype(f32)
    tab = jnp.tile(by_offset, (1, QA_BLOCK))[:, :QA_BLOCK * (period - 1)]
    tab = tab.reshape(h, QA_BLOCK, period - 1)[:, :, :BAND]
    q_chunk = np.arange(QA_BLOCK)[:, None] // CHUNK
    k_chunk = np.floor_divide(np.arange(BAND)[None, :] - PAD, CHUNK)
    in_band = (k_chunk <= q_chunk) & (k_chunk >= q_chunk - LEFT_CHUNKS)
    return jnp.where(jnp.asarray(in_band)[None], tab * LOG2E, NEG_BIG)


def kernel(x, norm1_g, w_in, q_norm_g, k_norm_g, rel_bias, w_branch_a, w_branch_b, w_out,
           norm2_g, w_ffn_up, ffn_conv_w, ffn_conv_b, w_ffn_down):
    b, s, d = x.shape
    depth = w_in.shape[0]
    head_of = np.arange(WIDTH) // HEAD_DIM
    seg = jnp.asarray(head_of[:, None] == head_of[None, :], dtype=bf16)
    for l in range(depth):
        qg = jnp.tile(q_norm_g[l].astype(f32), N_HEADS)[None, :]
        kg = jnp.tile(k_norm_g[l].astype(f32), N_HEADS)[None, :]
        proj = _in_proj(x.reshape(b * s, d), norm1_g[l][None, :], w_in[l].astype(bf16), seg, qg, kg)
        proj3 = proj.reshape(b, s, IN_WIDTH)
        out_a = _attn_a(proj3, _band_bias_table(rel_bias[l]))
        out_b = _attn_b(proj3)
        x1 = _mix(x.reshape(b * s, d), out_a.reshape(b * s, WIDTH), out_b.reshape(b * s, WIDTH),
                  proj, w_branch_a[l].astype(bf16), w_branch_b[l].astype(bf16),
                  w_out[l].astype(bf16))
        x = _ffn(x1.reshape(b, s, d), norm2_g[l][None, :], w_ffn_up[l].astype(bf16),
                 ffn_conv_w[l], ffn_conv_b[l][None, :], w_ffn_down[l].astype(bf16))
    return x
```

```python
import math

import jax
import jax.numpy as jnp
import numpy as np
from jax import lax
from jax.experimental import pallas as pl
from jax.experimental.pallas import tpu as pltpu

D_MODEL = 1024
CHUNK = 64
LEFT_CHUNKS = 8
HEAD_DIM = 64
N_HEADS = 8
WIDTH = N_HEADS * HEAD_DIM
MAX_REL = 128
D_FF = 2816
CONV_WIDTH = 3
EPS = 1e-6

LANES = 128
SUBLANES = 8
HEADS_PER_BLOCK = LANES // HEAD_DIM
N_PAIRS = N_HEADS // HEADS_PER_BLOCK
IN_WIDTH = 6 * WIDTH + 2 * D_MODEL

COL_QA, COL_KA, COL_VA = 0, WIDTH // LANES, 2 * WIDTH // LANES
COL_QB, COL_KB, COL_VB = 3 * WIDTH // LANES, 4 * WIDTH // LANES, 5 * WIDTH // LANES

ROW_TILE = 512
PROJ_CHUNK = 512
FFN_CHUNK = 256

QA_BLOCK = 128
BAND = LEFT_CHUNKS * CHUNK + QA_BLOCK
PAD = LEFT_CHUNKS * CHUNK
Q_TILE = 2048
A_UNROLL = 8

QB_BLOCK = 256
KB_BLOCK = QB_BLOCK
LOG2E = math.log2(math.e)
DEAD_LOG2 = -151.0
NEG_BIG = -1e30

VMEM_LIMIT = 56 * 1024 * 1024

f32 = jnp.float32
bf16 = jnp.bfloat16


def _resident(shape, index_map):
    return pl.BlockSpec(shape, index_map, pipeline_mode=pl.Buffered(1))


def _nt_dot(a, b):
    return lax.dot_general(a, b, (((1,), (1,)), ((), ())), preferred_element_type=f32)


def _dot(a, b):
    return jnp.dot(a, b, preferred_element_type=f32)


def _sigmoid(x):
    return 1.0 / (1.0 + jnp.exp(-x))


def _stack_heads(q2, lane):
    return jnp.concatenate(
        [jnp.where((lane >= h * HEAD_DIM) & (lane < (h + 1) * HEAD_DIM), q2, jnp.zeros_like(q2))
         for h in range(HEADS_PER_BLOCK)], axis=0)


def _in_proj_kernel(x_ref, g_ref, w_ref, seg_ref, qg_ref, kg_ref, o_ref):
    x = x_ref[...]
    ms = jnp.mean(x * x, axis=-1, keepdims=True)
    hn = (x * lax.rsqrt(ms + EPS) * g_ref[...]).astype(bf16)
    n_chunks = IN_WIDTH // PROJ_CHUNK
    gate_first = 6 * WIDTH // PROJ_CHUNK
    for c in range(n_chunks):
        cols = slice(c * PROJ_CHUNK, (c + 1) * PROJ_CHUNK)
        acc = _dot(hn, w_ref[:, cols])
        if c in (0, 1):
            ss = _dot((acc * acc).astype(bf16), seg_ref[...])
            gain = qg_ref[...] if c == 0 else kg_ref[...]
            acc = acc * lax.rsqrt(ss * (1.0 / HEAD_DIM) + EPS) * gain
        if c in (0, 3):
            acc = acc * (HEAD_DIM ** -0.5 * LOG2E)
        if c >= gate_first:
            acc = _sigmoid(acc)
        o_ref[:, cols] = acc.astype(bf16)


def _in_proj(x2d, norm_g, w_in, seg, qg, kg):
    t = x2d.shape[0]
    return pl.pallas_call(
        _in_proj_kernel,
        grid=(t // ROW_TILE,),
        in_specs=[
            pl.BlockSpec((ROW_TILE, D_MODEL), lambda i: (i, 0)),
            _resident((1, D_MODEL), lambda i: (0, 0)),
            _resident((D_MODEL, IN_WIDTH), lambda i: (0, 0)),
            _resident((WIDTH, WIDTH), lambda i: (0, 0)),
            _resident((1, WIDTH), lambda i: (0, 0)),
            _resident((1, WIDTH), lambda i: (0, 0)),
        ],
        out_specs=pl.BlockSpec((ROW_TILE, IN_WIDTH), lambda i: (i, 0)),
        out_shape=jax.ShapeDtypeStruct((t, IN_WIDTH), bf16),
        compiler_params=pltpu.CompilerParams(
            dimension_semantics=("parallel",), vmem_limit_bytes=VMEM_LIMIT),
        name="in_proj",
    )(x2d, norm_g, w_in, seg, qg, kg)


def _attn_a_kernel(q_ref, k_ref, v_ref, bias_ref, o_ref, kpad, vpad):
    t = pl.program_id(2)

    @pl.when(t == 0)
    def _():
        zeros = jnp.zeros((PAD, LANES), bf16)
        kpad[0:PAD, :] = zeros
        vpad[0:PAD, :] = zeros
        kpad[PAD:, :] = k_ref[...]
        vpad[PAD:, :] = v_ref[...]

    rows = HEADS_PER_BLOCK * QA_BLOCK
    lane = lax.broadcasted_iota(jnp.int32, (QA_BLOCK, LANES), 1)
    col = lax.broadcasted_iota(jnp.int32, (rows, BAND), 1)
    blocks_per_tile = Q_TILE // QA_BLOCK

    def make_body(band_has_padding):
        def body(ii, carry):
            qi = t * blocks_per_tile + ii
            r0 = pl.multiple_of(ii * QA_BLOCK, QA_BLOCK)
            b0 = pl.multiple_of(qi * QA_BLOCK, QA_BLOCK)
            qs = _stack_heads(q_ref[pl.ds(r0, QA_BLOCK), :], lane)
            kb = kpad[pl.ds(b0, BAND), :]
            vb = vpad[pl.ds(b0, BAND), :]
            s = _nt_dot(qs, kb) + bias_ref[...].reshape(rows, BAND)
            if band_has_padding:
                s = jnp.where(col < (PAD - qi * QA_BLOCK), NEG_BIG, s)
            m = jnp.max(s, axis=-1, keepdims=True)
            p = jnp.exp2(s - m)
            l = jnp.sum(p, axis=-1, keepdims=True)
            o = _dot(p.astype(bf16), vb) * (1.0 / l)
            o = jnp.where(lane < HEAD_DIM, o[:QA_BLOCK], o[QA_BLOCK:])
            o_ref[pl.ds(r0, QA_BLOCK), :] = o.astype(bf16)
            return carry
        return body

    padded_blocks = PAD // QA_BLOCK

    @pl.when(t == 0)
    def _():
        lax.fori_loop(0, padded_blocks, make_body(True), 0, unroll=A_UNROLL)
        lax.fori_loop(padded_blocks, blocks_per_tile, make_body(False), 0, unroll=A_UNROLL)

    @pl.when(t != 0)
    def _():
        lax.fori_loop(0, blocks_per_tile, make_body(False), 0, unroll=A_UNROLL)


def _attn_a(proj3, bias_tab):
    b, s, _ = proj3.shape
    return pl.pallas_call(
        _attn_a_kernel,
        grid=(b, N_PAIRS, s // Q_TILE),
        in_specs=[
            pl.BlockSpec((None, Q_TILE, LANES), lambda bi, p, t: (bi, t, COL_QA + p)),
            pl.BlockSpec((None, s, LANES), lambda bi, p, t: (bi, 0, COL_KA + p)),
            pl.BlockSpec((None, s, LANES), lambda bi, p, t: (bi, 0, COL_VA + p)),
            pl.BlockSpec((HEADS_PER_BLOCK, QA_BLOCK, BAND), lambda bi, p, t: (p, 0, 0)),
        ],
        out_specs=pl.BlockSpec((None, Q_TILE, LANES), lambda bi, p, t: (bi, t, p)),
        out_shape=jax.ShapeDtypeStruct((b, s, WIDTH), bf16),
        scratch_shapes=[pltpu.VMEM((PAD + s, LANES), bf16),
                        pltpu.VMEM((PAD + s, LANES), bf16)],
        compiler_params=pltpu.CompilerParams(
            dimension_semantics=("parallel", "parallel", "arbitrary"),
            vmem_limit_bytes=VMEM_LIMIT),
        name="attn_a",
    )(proj3, proj3, proj3, bias_tab)


def _neg_abs(x):
    bits = lax.bitcast_convert_type(x, jnp.uint32) | jnp.uint32(0x80000000)
    return lax.bitcast_convert_type(bits, f32)


def _attn_b_kernel(q_ref, k_ref, v_ref, o_ref, acc_ref, carry_ref):
    t = pl.program_id(2)
    rows = HEADS_PER_BLOCK * QB_BLOCK
    lane = lax.broadcasted_iota(jnp.int32, (QB_BLOCK, LANES), 1)
    row = lax.broadcasted_iota(jnp.int32, (rows, KB_BLOCK), 0)
    col = lax.broadcasted_iota(jnp.int32, (rows, KB_BLOCK), 1)
    causal = col < (row & (QB_BLOCK - 1))
    neg_tri = jnp.where(lax.broadcasted_iota(jnp.int32, (KB_BLOCK, KB_BLOCK), 0)
                        >= lax.broadcasted_iota(jnp.int32, (KB_BLOCK, KB_BLOCK), 1),
                        -1.0, 0.0).astype(bf16)
    blocks_per_tile = Q_TILE // QB_BLOCK

    def scores(qs, k0):
        z = _nt_dot(qs, k_ref[pl.ds(k0, KB_BLOCK), :])
        sp = jnp.maximum(z, 0.0) + jnp.log2(1.0 + jnp.exp2(_neg_abs(z)))
        return z, sp

    def weights(z, sp, diag):
        log_keep_from = _dot(sp.astype(bf16), neg_tri)
        w = jnp.exp2(z + log_keep_from)
        if diag:
            w = jnp.where(causal, w, 0.0)
        return w.astype(bf16), log_keep_from[:, 0:1]

    def alive_of(c):
        return (jnp.max(c) > DEAD_LOG2).astype(jnp.int32)

    def load_q(ii):
        r0 = pl.multiple_of(ii * QB_BLOCK, QB_BLOCK)
        q0 = pl.multiple_of(t * Q_TILE + ii * QB_BLOCK, QB_BLOCK)
        return q0, _stack_heads(q_ref[pl.ds(r0, QB_BLOCK), :], lane)

    def emit(ii, acc):
        r0 = pl.multiple_of(ii * QB_BLOCK, QB_BLOCK)
        o = jnp.where(lane < HEAD_DIM, acc[:QB_BLOCK], acc[QB_BLOCK:])
        o_ref[pl.ds(r0, QB_BLOCK), :] = o.astype(bf16)

    def first_pass(ii, loop_carry):
        q0, qs = load_q(ii)
        z_d, sp_d = scores(qs, q0)
        w_d, tot_d = weights(z_d, jnp.where(causal, sp_d, 0.0), True)
        has_prev = q0 > 0
        k_prev = pl.multiple_of(jnp.maximum(q0 - KB_BLOCK, 0), KB_BLOCK)
        z_p, sp_p = scores(jnp.where(has_prev, qs, jnp.zeros_like(qs)), k_prev)
        w_p, tot_p = weights(z_p, sp_p, False)
        scale_p = jnp.where(has_prev, jnp.exp2(tot_d), 0.0)
        acc = (_dot(w_d, v_ref[pl.ds(q0, KB_BLOCK), :])
               + _dot(w_p, v_ref[pl.ds(k_prev, KB_BLOCK), :]) * scale_p)
        acc_ref[ii] = acc
        carry_ref[ii] = tot_d + tot_p
        emit(ii, acc)
        return loop_carry

    lax.fori_loop(0, blocks_per_tile, first_pass, 0, unroll=4)

    def older_blocks(ii, loop_carry):
        q0, qs = load_q(ii)

        def cond(st):
            j, _, _, alive = st
            return jnp.logical_and(j >= 0, alive > 0)

        def sweep(st):
            j, c, a, _ = st
            k0 = pl.multiple_of(j * KB_BLOCK, KB_BLOCK)
            z, sp = scores(qs, k0)
            w, tot = weights(z, sp, False)
            a = a + _dot(w, v_ref[pl.ds(k0, KB_BLOCK), :]) * jnp.exp2(c)
            c = c + tot
            return j - 1, c, a, alive_of(c)

        carry = carry_ref[ii]
        _, _, acc, _ = lax.while_loop(
            cond, sweep, (q0 // KB_BLOCK - 2, carry, acc_ref[ii], alive_of(carry)))
        emit(ii, acc)
        return loop_carry

    @pl.when(jnp.max(carry_ref[...]) > DEAD_LOG2)
    def _():
        lax.fori_loop(0, blocks_per_tile, older_blocks, 0)


def _attn_b(proj3):
    b, s, _ = proj3.shape
    return pl.pallas_call(
        _attn_b_kernel,
        grid=(b, N_PAIRS, s // Q_TILE),
        in_specs=[
            pl.BlockSpec((None, Q_TILE, LANES), lambda bi, p, t: (bi, t, COL_QB + p)),
            pl.BlockSpec((None, s, LANES), lambda bi, p, t: (bi, 0, COL_KB + p)),
            pl.BlockSpec((None, s, LANES), lambda bi, p, t: (bi, 0, COL_VB + p)),
        ],
        out_specs=pl.BlockSpec((None, Q_TILE, LANES), lambda bi, p, t: (bi, t, p)),
        out_shape=jax.ShapeDtypeStruct((b, s, WIDTH), bf16),
        scratch_shapes=[
            pltpu.VMEM((Q_TILE // QB_BLOCK, HEADS_PER_BLOCK * QB_BLOCK, LANES), f32),
            pltpu.VMEM((Q_TILE // QB_BLOCK, HEADS_PER_BLOCK * QB_BLOCK, 1), f32)],
        compiler_params=pltpu.CompilerParams(
            dimension_semantics=("parallel", "parallel", "arbitrary"),
            vmem_limit_bytes=VMEM_LIMIT),
        name="attn_b",
    )(proj3, proj3, proj3)


def _mix_ffn_kernel(x_ref, oa_ref, ob_ref, ga_ref, gb_ref, wa_ref, wb_ref, wo_ref, g_ref,
                    wup_ref, cw_ref, cb_ref, wdn_ref, o_ref, tail_ref, hbuf, act_ref):
    @pl.when(pl.program_id(1) == 0)
    def _():
        tail_ref[...] = jnp.zeros_like(tail_ref)

    ya = _dot(oa_ref[...], wa_ref[...])
    yb = _dot(ob_ref[...], wb_ref[...])
    mixed = ga_ref[...].astype(f32) * ya + gb_ref[...].astype(f32) * yb
    x = x_ref[...] + _dot(mixed.astype(bf16), wo_ref[...])
    o_ref[...] = x
    ms = jnp.mean(x * x, axis=-1, keepdims=True)
    hn = (x * lax.rsqrt(ms + EPS) * g_ref[...]).astype(bf16)
    for c in range(D_FF // FFN_CHUNK):
        halves = []
        for half in range(2):
            c0 = half * D_FF + c * FFN_CHUNK
            cols = slice(c0, c0 + FFN_CHUNK)
            h = _dot(hn, wup_ref[:, cols])
            hbuf[half, 0:SUBLANES, :] = tail_ref[:, cols]
            hbuf[half, SUBLANES:, :] = h
            tail_ref[:, cols] = h[ROW_TILE - SUBLANES:, :]
            y = cb_ref[:, cols] + cw_ref[CONV_WIDTH - 1:CONV_WIDTH, cols] * h
            for back in range(1, CONV_WIDTH):
                shifted = hbuf[half, SUBLANES - back:SUBLANES - back + ROW_TILE, :]
                y = y + cw_ref[CONV_WIDTH - 1 - back:CONV_WIDTH - back, cols] * shifted
            halves.append(y)
        gate, up = halves
        act_ref[:, c * FFN_CHUNK:(c + 1) * FFN_CHUNK] = (gate * _sigmoid(gate) * up).astype(bf16)
    o_ref[...] += _dot(act_ref[...], wdn_ref[...])


def _mix_ffn(x3, oa, ob, proj3, wa, wb, wo, norm_g, w_up, conv_w, conv_b, w_down):
    b, s, _ = x3.shape
    gate_a_blk = 6 * WIDTH // D_MODEL
    row_spec = lambda w, blk: pl.BlockSpec((None, ROW_TILE, w), lambda bi, i: (bi, i, blk))
    const = lambda shape: _resident(shape, lambda bi, i: (0, 0))
    return pl.pallas_call(
        _mix_ffn_kernel,
        grid=(b, s // ROW_TILE),
        in_specs=[
            row_spec(D_MODEL, 0), row_spec(WIDTH, 0), row_spec(WIDTH, 0),
            row_spec(D_MODEL, gate_a_blk), row_spec(D_MODEL, gate_a_blk + 1),
            const((WIDTH, D_MODEL)), const((WIDTH, D_MODEL)), const((D_MODEL, D_MODEL)),
            const((1, D_MODEL)), const((D_MODEL, 2 * D_FF)), const((CONV_WIDTH, 2 * D_FF)),
            const((1, 2 * D_FF)), const((D_FF, D_MODEL)),
        ],
        out_specs=row_spec(D_MODEL, 0),
        out_shape=jax.ShapeDtypeStruct((b, s, D_MODEL), f32),
        scratch_shapes=[pltpu.VMEM((SUBLANES, 2 * D_FF), f32),
                        pltpu.VMEM((2, SUBLANES + ROW_TILE, FFN_CHUNK), f32),
                        pltpu.VMEM((ROW_TILE, D_FF), bf16)],
        compiler_params=pltpu.CompilerParams(
            dimension_semantics=("parallel", "arbitrary"), vmem_limit_bytes=VMEM_LIMIT),
        name="mix_ffn",
    )(x3, oa, ob, proj3, proj3, wa, wb, wo, norm_g, w_up, conv_w, conv_b, w_down)


def _band_bias_table(rel_bias):
    h = rel_bias.shape[0]
    period = BAND + QA_BLOCK
    far = rel_bias[:, 2 * MAX_REL:]
    by_offset = jnp.concatenate([
        jnp.broadcast_to(far, (h, PAD - MAX_REL + 1)),
        rel_bias[:, 2 * MAX_REL - 1:0:-1],
        jnp.zeros((h, 1), rel_bias.dtype),
        jnp.broadcast_to(far, (h, QA_BLOCK - 1)),
    ], axis=1).astype(f32)
    tab = jnp.tile(by_offset, (1, QA_BLOCK))[:, :QA_BLOCK * (period - 1)]
    tab = tab.reshape(h, QA_BLOCK, period - 1)[:, :, :BAND]
    q_chunk = np.arange(QA_BLOCK)[:, None] // CHUNK
    k_chunk = np.floor_divide(np.arange(BAND)[None, :] - PAD, CHUNK)
    in_band = (k_chunk <= q_chunk) & (k_chunk >= q_chunk - LEFT_CHUNKS)
    return jnp.where(jnp.asarray(in_band)[None], tab * LOG2E, NEG_BIG)


def kernel(x, norm1_g, w_in, q_norm_g, k_norm_g, rel_bias, w_branch_a, w_branch_b, w_out,
           norm2_g, w_ffn_up, ffn_conv_w, ffn_conv_b, w_ffn_down):
    b, s, d = x.shape
    depth = w_in.shape[0]
    head_of = np.arange(WIDTH) // HEAD_DIM
    seg = jnp.asarray(head_of[:, None] == head_of[None, :], dtype=bf16)
    for l in range(depth):
        qg = jnp.tile(q_norm_g[l].astype(f32), N_HEADS)[None, :]
        kg = jnp.tile(k_norm_g[l].astype(f32), N_HEADS)[None, :]
        proj = _in_proj(x.reshape(b * s, d), norm1_g[l][None, :], w_in[l].astype(bf16), seg, qg, kg)
        proj3 = proj.reshape(b, s, IN_WIDTH)
        out_a = _attn_a(proj3, _band_bias_table(rel_bias[l]))
        out_b = _attn_b(proj3)
        x = _mix_ffn(x, out_a, out_b, proj3, w_branch_a[l].astype(bf16), w_branch_b[l].astype(bf16),
                     w_out[l].astype(bf16), norm2_g[l][None, :], w_ffn_up[l].astype(bf16),
                     ffn_conv_w[l], ffn_conv_b[l][None, :], w_ffn_down[l].astype(bf16))
    return x
```

```python
import math

import jax
import jax.numpy as jnp
import numpy as np
from jax import lax
from jax.experimental import pallas as pl
from jax.experimental.pallas import tpu as pltpu

D_MODEL = 1024
CHUNK = 64
LEFT_CHUNKS = 8
HEAD_DIM = 64
N_HEADS = 8
WIDTH = N_HEADS * HEAD_DIM
MAX_REL = 128
D_FF = 2816
CONV_WIDTH = 3
EPS = 1e-6

LANES = 128
SUBLANES = 8
HEADS_PER_BLOCK = LANES // HEAD_DIM
N_PAIRS = N_HEADS // HEADS_PER_BLOCK
IN_WIDTH = 6 * WIDTH + 2 * D_MODEL

COL_QA, COL_KA, COL_VA = 0, WIDTH // LANES, 2 * WIDTH // LANES
COL_QB, COL_KB, COL_VB = 3 * WIDTH // LANES, 4 * WIDTH // LANES, 5 * WIDTH // LANES

ROW_TILE = 512
PROJ_CHUNK = 512
FFN_CHUNK = 256

QA_BLOCK = 128
BAND = LEFT_CHUNKS * CHUNK + QA_BLOCK
PAD = LEFT_CHUNKS * CHUNK
Q_TILE = 2048
A_UNROLL = 8
SCORE_BOUND = 60.0

QB_BLOCK = 256
KB_BLOCK = QB_BLOCK
LOG2E = math.log2(math.e)
DEAD_LOG2 = -151.0
NEG_BIG = -1e30

VMEM_LIMIT = 56 * 1024 * 1024

f32 = jnp.float32
bf16 = jnp.bfloat16


def _resident(shape, index_map):
    return pl.BlockSpec(shape, index_map, pipeline_mode=pl.Buffered(1))


def _nt_dot(a, b):
    return lax.dot_general(a, b, (((1,), (1,)), ((), ())), preferred_element_type=f32)


def _dot(a, b):
    return jnp.dot(a, b, preferred_element_type=f32)


def _sigmoid(x):
    return 1.0 / (1.0 + jnp.exp(-x))


def _stack_heads(q2, lane):
    return jnp.concatenate(
        [jnp.where((lane >= h * HEAD_DIM) & (lane < (h + 1) * HEAD_DIM), q2, jnp.zeros_like(q2))
         for h in range(HEADS_PER_BLOCK)], axis=0)


def _in_proj_kernel(x_ref, g_ref, w_ref, seg_ref, qg_ref, kg_ref, o_ref):
    x = x_ref[...]
    ms = jnp.mean(x * x, axis=-1, keepdims=True)
    hn = (x * lax.rsqrt(ms + EPS) * g_ref[...]).astype(bf16)
    n_chunks = IN_WIDTH // PROJ_CHUNK
    gate_first = 6 * WIDTH // PROJ_CHUNK
    for c in range(n_chunks):
        cols = slice(c * PROJ_CHUNK, (c + 1) * PROJ_CHUNK)
        acc = _dot(hn, w_ref[:, cols])
        if c in (0, 1):
            ss = _dot((acc * acc).astype(bf16), seg_ref[...])
            gain = qg_ref[...] if c == 0 else kg_ref[...]
            acc = acc * lax.rsqrt(ss * (1.0 / HEAD_DIM) + EPS) * gain
        if c in (0, 3):
            acc = acc * (HEAD_DIM ** -0.5 * LOG2E)
        if c >= gate_first:
            acc = _sigmoid(acc)
        o_ref[:, cols] = acc.astype(bf16)


def _in_proj(x2d, norm_g, w_in, seg, qg, kg):
    t = x2d.shape[0]
    return pl.pallas_call(
        _in_proj_kernel,
        grid=(t // ROW_TILE,),
        in_specs=[
            pl.BlockSpec((ROW_TILE, D_MODEL), lambda i: (i, 0)),
            _resident((1, D_MODEL), lambda i: (0, 0)),
            _resident((D_MODEL, IN_WIDTH), lambda i: (0, 0)),
            _resident((WIDTH, WIDTH), lambda i: (0, 0)),
            _resident((1, WIDTH), lambda i: (0, 0)),
            _resident((1, WIDTH), lambda i: (0, 0)),
        ],
        out_specs=pl.BlockSpec((ROW_TILE, IN_WIDTH), lambda i: (i, 0)),
        out_shape=jax.ShapeDtypeStruct((t, IN_WIDTH), bf16),
        compiler_params=pltpu.CompilerParams(
            dimension_semantics=("parallel",), vmem_limit_bytes=VMEM_LIMIT),
        name="in_proj",
    )(x2d, norm_g, w_in, seg, qg, kg)


def _attn_a_kernel(bounded_ref, q_ref, k_ref, v_ref, bias_ref, o_ref, kpad, vpad):
    t = pl.program_id(2)

    @pl.when(t == 0)
    def _():
        zeros = jnp.zeros((PAD, LANES), bf16)
        kpad[0:PAD, :] = zeros
        vpad[0:PAD, :] = zeros
        kpad[PAD:, :] = k_ref[...]
        vpad[PAD:, :] = v_ref[...]

    rows = HEADS_PER_BLOCK * QA_BLOCK
    lane = lax.broadcasted_iota(jnp.int32, (QA_BLOCK, LANES), 1)
    col = lax.broadcasted_iota(jnp.int32, (rows, BAND), 1)
    blocks_per_tile = Q_TILE // QA_BLOCK

    def make_body(band_has_padding, bounded):
        def body(ii, carry):
            qi = t * blocks_per_tile + ii
            r0 = pl.multiple_of(ii * QA_BLOCK, QA_BLOCK)
            b0 = pl.multiple_of(qi * QA_BLOCK, QA_BLOCK)
            qs = _stack_heads(q_ref[pl.ds(r0, QA_BLOCK), :], lane)
            kb = kpad[pl.ds(b0, BAND), :]
            vb = vpad[pl.ds(b0, BAND), :]
            s = _nt_dot(qs, kb) + bias_ref[...].reshape(rows, BAND)
            if band_has_padding:
                s = jnp.where(col < (PAD - qi * QA_BLOCK), NEG_BIG, s)
            if bounded:
                p = jnp.exp2(s)
            else:
                p = jnp.exp2(s - jnp.max(s, axis=-1, keepdims=True))
            l = jnp.sum(p, axis=-1, keepdims=True)
            o = _dot(p.astype(bf16), vb) * (1.0 / l)
            o = jnp.where(lane < HEAD_DIM, o[:QA_BLOCK], o[QA_BLOCK:])
            o_ref[pl.ds(r0, QA_BLOCK), :] = o.astype(bf16)
            return carry
        return body

    padded_blocks = PAD // QA_BLOCK

    def sweep(bounded):
        @pl.when(t == 0)
        def _():
            lax.fori_loop(0, padded_blocks, make_body(True, bounded), 0, unroll=A_UNROLL)
            lax.fori_loop(padded_blocks, blocks_per_tile, make_body(False, bounded), 0,
                          unroll=A_UNROLL)

        @pl.when(t != 0)
        def _():
            lax.fori_loop(0, blocks_per_tile, make_body(False, bounded), 0, unroll=A_UNROLL)

    is_bounded = bounded_ref[0] != 0
    pl.when(is_bounded)(lambda: sweep(True))
    pl.when(jnp.logical_not(is_bounded))(lambda: sweep(False))


def _attn_a(proj3, bias_tab, bounded):
    b, s, _ = proj3.shape
    return pl.pallas_call(
        _attn_a_kernel,
        grid=(b, N_PAIRS, s // Q_TILE),
        in_specs=[
            pl.BlockSpec(memory_space=pltpu.SMEM),
            pl.BlockSpec((None, Q_TILE, LANES), lambda bi, p, t: (bi, t, COL_QA + p)),
            pl.BlockSpec((None, s, LANES), lambda bi, p, t: (bi, 0, COL_KA + p)),
            pl.BlockSpec((None, s, LANES), lambda bi, p, t: (bi, 0, COL_VA + p)),
            pl.BlockSpec((HEADS_PER_BLOCK, QA_BLOCK, BAND), lambda bi, p, t: (p, 0, 0)),
        ],
        out_specs=pl.BlockSpec((None, Q_TILE, LANES), lambda bi, p, t: (bi, t, p)),
        out_shape=jax.ShapeDtypeStruct((b, s, WIDTH), bf16),
        scratch_shapes=[pltpu.VMEM((PAD + s, LANES), bf16),
                        pltpu.VMEM((PAD + s, LANES), bf16)],
        compiler_params=pltpu.CompilerParams(
            dimension_semantics=("parallel", "parallel", "arbitrary"),
            vmem_limit_bytes=VMEM_LIMIT),
        name="attn_a",
    )(bounded, proj3, proj3, proj3, bias_tab)


def _neg_abs(x):
    bits = lax.bitcast_convert_type(x, jnp.uint32) | jnp.uint32(0x80000000)
    return lax.bitcast_convert_type(bits, f32)


def _attn_b_kernel(q_ref, k_ref, v_ref, o_ref, acc_ref, carry_ref):
    t = pl.program_id(2)
    rows = HEADS_PER_BLOCK * QB_BLOCK
    lane = lax.broadcasted_iota(jnp.int32, (QB_BLOCK, LANES), 1)
    row = lax.broadcasted_iota(jnp.int32, (rows, KB_BLOCK), 0)
    col = lax.broadcasted_iota(jnp.int32, (rows, KB_BLOCK), 1)
    causal = col < (row & (QB_BLOCK - 1))
    neg_tri = jnp.where(lax.broadcasted_iota(jnp.int32, (KB_BLOCK, KB_BLOCK), 0)
                        >= lax.broadcasted_iota(jnp.int32, (KB_BLOCK, KB_BLOCK), 1),
                        -1.0, 0.0).astype(bf16)
    blocks_per_tile = Q_TILE // QB_BLOCK

    def scores(qs, k0):
        z = _nt_dot(qs, k_ref[pl.ds(k0, KB_BLOCK), :])
        sp = jnp.maximum(z, 0.0) + jnp.log2(1.0 + jnp.exp2(_neg_abs(z)))
        return z, sp

    def weights(z, sp, diag):
        log_keep_from = _dot(sp.astype(bf16), neg_tri)
        w = jnp.exp2(z + log_keep_from)
        if diag:
            w = jnp.where(causal, w, 0.0)
        return w.astype(bf16), log_keep_from[:, 0:1]

    def alive_of(c):
        return (jnp.max(c) > DEAD_LOG2).astype(jnp.int32)

    def load_q(ii):
        r0 = pl.multiple_of(ii * QB_BLOCK, QB_BLOCK)
        q0 = pl.multiple_of(t * Q_TILE + ii * QB_BLOCK, QB_BLOCK)
        return q0, _stack_heads(q_ref[pl.ds(r0, QB_BLOCK), :], lane)

    def emit(ii, acc):
        r0 = pl.multiple_of(ii * QB_BLOCK, QB_BLOCK)
        o = jnp.where(lane < HEAD_DIM, acc[:QB_BLOCK], acc[QB_BLOCK:])
        o_ref[pl.ds(r0, QB_BLOCK), :] = o.astype(bf16)

    def first_pass(ii, loop_carry):
        q0, qs = load_q(ii)
        z_d, sp_d = scores(qs, q0)
        w_d, tot_d = weights(z_d, jnp.where(causal, sp_d, 0.0), True)
        has_prev = q0 > 0
        k_prev = pl.multiple_of(jnp.maximum(q0 - KB_BLOCK, 0), KB_BLOCK)
        z_p, sp_p = scores(jnp.where(has_prev, qs, jnp.zeros_like(qs)), k_prev)
        w_p, tot_p = weights(z_p, sp_p, False)
        scale_p = jnp.where(has_prev, jnp.exp2(tot_d), 0.0)
        acc = (_dot(w_d, v_ref[pl.ds(q0, KB_BLOCK), :])
               + _dot(w_p, v_ref[pl.ds(k_prev, KB_BLOCK), :]) * scale_p)
        acc_ref[ii] = acc
        carry_ref[ii] = tot_d + tot_p
        emit(ii, acc)
        return loop_carry

    lax.fori_loop(0, blocks_per_tile, first_pass, 0, unroll=4)

    def older_blocks(ii, loop_carry):
        q0, qs = load_q(ii)

        def cond(st):
            j, _, _, alive = st
            return jnp.logical_and(j >= 0, alive > 0)

        def sweep(st):
            j, c, a, _ = st
            k0 = pl.multiple_of(j * KB_BLOCK, KB_BLOCK)
            z, sp = scores(qs, k0)
            w, tot = weights(z, sp, False)
            a = a + _dot(w, v_ref[pl.ds(k0, KB_BLOCK), :]) * jnp.exp2(c)
            c = c + tot
            return j - 1, c, a, alive_of(c)

        carry = carry_ref[ii]
        _, _, acc, _ = lax.while_loop(
            cond, sweep, (q0 // KB_BLOCK - 2, carry, acc_ref[ii], alive_of(carry)))
        emit(ii, acc)
        return loop_carry

    @pl.when(jnp.max(carry_ref[...]) > DEAD_LOG2)
    def _():
        lax.fori_loop(0, blocks_per_tile, older_blocks, 0)


def _attn_b(proj3):
    b, s, _ = proj3.shape
    return pl.pallas_call(
        _attn_b_kernel,
        grid=(b, N_PAIRS, s // Q_TILE),
        in_specs=[
            pl.BlockSpec((None, Q_TILE, LANES), lambda bi, p, t: (bi, t, COL_QB + p)),
            pl.BlockSpec((None, s, LANES), lambda bi, p, t: (bi, 0, COL_KB + p)),
            pl.BlockSpec((None, s, LANES), lambda bi, p, t: (bi, 0, COL_VB + p)),
        ],
        out_specs=pl.BlockSpec((None, Q_TILE, LANES), lambda bi, p, t: (bi, t, p)),
        out_shape=jax.ShapeDtypeStruct((b, s, WIDTH), bf16),
        scratch_shapes=[
            pltpu.VMEM((Q_TILE // QB_BLOCK, HEADS_PER_BLOCK * QB_BLOCK, LANES), f32),
            pltpu.VMEM((Q_TILE // QB_BLOCK, HEADS_PER_BLOCK * QB_BLOCK, 1), f32)],
        compiler_params=pltpu.CompilerParams(
            dimension_semantics=("parallel", "parallel", "arbitrary"),
            vmem_limit_bytes=VMEM_LIMIT),
        name="attn_b",
    )(proj3, proj3, proj3)


def _mix_ffn_kernel(x_ref, oa_ref, ob_ref, ga_ref, gb_ref, wa_ref, wb_ref, wo_ref, g_ref,
                    wup_ref, cw_ref, cb_ref, wdn_ref, o_ref, tail_ref, hbuf, act_ref):
    @pl.when(pl.program_id(1) == 0)
    def _():
        tail_ref[...] = jnp.zeros_like(tail_ref)

    ya = _dot(oa_ref[...], wa_ref[...])
    yb = _dot(ob_ref[...], wb_ref[...])
    mixed = ga_ref[...].astype(f32) * ya + gb_ref[...].astype(f32) * yb
    x = x_ref[...] + _dot(mixed.astype(bf16), wo_ref[...])
    o_ref[...] = x
    ms = jnp.mean(x * x, axis=-1, keepdims=True)
    hn = (x * lax.rsqrt(ms + EPS) * g_ref[...]).astype(bf16)
    for c in range(D_FF // FFN_CHUNK):
        halves = []
        for half in range(2):
            c0 = half * D_FF + c * FFN_CHUNK
            cols = slice(c0, c0 + FFN_CHUNK)
            h = _dot(hn, wup_ref[:, cols])
            hbuf[half, 0:SUBLANES, :] = tail_ref[:, cols]
            hbuf[half, SUBLANES:, :] = h
            tail_ref[:, cols] = h[ROW_TILE - SUBLANES:, :]
            y = cb_ref[:, cols] + cw_ref[CONV_WIDTH - 1:CONV_WIDTH, cols] * h
            for back in range(1, CONV_WIDTH):
                shifted = hbuf[half, SUBLANES - back:SUBLANES - back + ROW_TILE, :]
                y = y + cw_ref[CONV_WIDTH - 1 - back:CONV_WIDTH - back, cols] * shifted
            halves.append(y)
        gate, up = halves
        act_ref[:, c * FFN_CHUNK:(c + 1) * FFN_CHUNK] = (gate * _sigmoid(gate) * up).astype(bf16)
    o_ref[...] += _dot(act_ref[...], wdn_ref[...])


def _mix_ffn(x3, oa, ob, proj3, wa, wb, wo, norm_g, w_up, conv_w, conv_b, w_down):
    b, s, _ = x3.shape
    gate_a_blk = 6 * WIDTH // D_MODEL
    row_spec = lambda w, blk: pl.BlockSpec((None, ROW_TILE, w), lambda bi, i: (bi, i, blk))
    const = lambda shape: _resident(shape, lambda bi, i: (0, 0))
    return pl.pallas_call(
        _mix_ffn_kernel,
        grid=(b, s // ROW_TILE),
        in_specs=[
            row_spec(D_MODEL, 0), row_spec(WIDTH, 0), row_spec(WIDTH, 0),
            row_spec(D_MODEL, gate_a_blk), row_spec(D_MODEL, gate_a_blk + 1),
            const((WIDTH, D_MODEL)), const((WIDTH, D_MODEL)), const((D_MODEL, D_MODEL)),
            const((1, D_MODEL)), const((D_MODEL, 2 * D_FF)), const((CONV_WIDTH, 2 * D_FF)),
            const((1, 2 * D_FF)), const((D_FF, D_MODEL)),
        ],
        out_specs=row_spec(D_MODEL, 0),
        out_shape=jax.ShapeDtypeStruct((b, s, D_MODEL), f32),
        scratch_shapes=[pltpu.VMEM((SUBLANES, 2 * D_FF), f32),
                        pltpu.VMEM((2, SUBLANES + ROW_TILE, FFN_CHUNK), f32),
                        pltpu.VMEM((ROW_TILE, D_FF), bf16)],
        compiler_params=pltpu.CompilerParams(
            dimension_semantics=("parallel", "arbitrary"), vmem_limit_bytes=VMEM_LIMIT),
        name="mix_ffn",
    )(x3, oa, ob, proj3, proj3, wa, wb, wo, norm_g, w_up, conv_w, conv_b, w_down)


def _band_bias_table(rel_bias):
    h = rel_bias.shape[0]
    period = BAND + QA_BLOCK
    far = rel_bias[:, 2 * MAX_REL:]
    by_offset = jnp.concatenate([
        jnp.broadcast_to(far, (h, PAD - MAX_REL + 1)),
        rel_bias[:, 2 * MAX_REL - 1:0:-1],
        jnp.zeros((h, 1), rel_bias.dtype),
        jnp.broadcast_to(far, (h, QA_BLOCK - 1)),
    ], axis=1).astype(f32)
    tab = jnp.tile(by_offset, (1, QA_BLOCK))[:, :QA_BLOCK * (period - 1)]
    tab = tab.reshape(h, QA_BLOCK, period - 1)[:, :, :BAND]
    q_chunk = np.arange(QA_BLOCK)[:, None] // CHUNK
    k_chunk = np.floor_divide(np.arange(BAND)[None, :] - PAD, CHUNK)
    in_band = (k_chunk <= q_chunk) & (k_chunk >= q_chunk - LEFT_CHUNKS)
    return jnp.asarray(in_band)[None], tab * LOG2E


def _score_bound(q_gain, k_gain, rel_bias):
    dot_bound = HEAD_DIM ** 0.5 * jnp.max(jnp.abs(q_gain)) * jnp.max(jnp.abs(k_gain))
    return LOG2E * (1.02 * dot_bound + jnp.max(jnp.abs(rel_bias)))


def kernel(x, norm1_g, w_in, q_norm_g, k_norm_g, rel_bias, w_branch_a, w_branch_b, w_out,
           norm2_g, w_ffn_up, ffn_conv_w, ffn_conv_b, w_ffn_down):
    b, s, d = x.shape
    depth = w_in.shape[0]
    head_of = np.arange(WIDTH) // HEAD_DIM
    seg = jnp.asarray(head_of[:, None] == head_of[None, :], dtype=bf16)
    for l in range(depth):
        qg = jnp.tile(q_norm_g[l].astype(f32), N_HEADS)[None, :]
        kg = jnp.tile(k_norm_g[l].astype(f32), N_HEADS)[None, :]
        proj = _in_proj(x.reshape(b * s, d), norm1_g[l][None, :], w_in[l].astype(bf16), seg, qg, kg)
        proj3 = proj.reshape(b, s, IN_WIDTH)
        bound = _score_bound(q_norm_g[l].astype(f32), k_norm_g[l].astype(f32), rel_bias[l].astype(f32))
        bounded = bound <= SCORE_BOUND
        in_band, tab = _band_bias_table(rel_bias[l])
        tab = jnp.where(in_band, tab - jnp.where(bounded, bound, 0.0), NEG_BIG)
        out_a = _attn_a(proj3, tab, bounded.astype(jnp.int32).reshape(1))
        out_b = _attn_b(proj3)
        x = _mix_ffn(x, out_a, out_b, proj3, w_branch_a[l].astype(bf16), w_branch_b[l].astype(bf16),
                     w_out[l].astype(bf16), norm2_g[l][None, :], w_ffn_up[l].astype(bf16),
                     ffn_conv_w[l], ffn_conv_b[l][None, :], w_ffn_down[l].astype(bf16))
    return x
```

```python
import math

import jax
import jax.numpy as jnp
import numpy as np
from jax import lax
from jax.experimental import pallas as pl
from jax.experimental.pallas import tpu as pltpu

D_MODEL = 1024
CHUNK = 64
LEFT_CHUNKS = 8
HEAD_DIM = 64
N_HEADS = 8
WIDTH = N_HEADS * HEAD_DIM
MAX_REL = 128
D_FF = 2816
CONV_WIDTH = 3
EPS = 1e-6

LANES = 128
SUBLANES = 8
HEADS_PER_BLOCK = LANES // HEAD_DIM
N_PAIRS = N_HEADS // HEADS_PER_BLOCK
IN_WIDTH = 6 * WIDTH + 2 * D_MODEL

COL_QA, COL_KA, COL_VA = 0, WIDTH // LANES, 2 * WIDTH // LANES
COL_QB, COL_KB, COL_VB = 3 * WIDTH // LANES, 4 * WIDTH // LANES, 5 * WIDTH // LANES

ROW_TILE = 512
PROJ_CHUNK = 512
FFN_CHUNK = 256

QA_BLOCK = 128
BAND = LEFT_CHUNKS * CHUNK + QA_BLOCK
PAD = LEFT_CHUNKS * CHUNK
Q_TILE = 2048
A_UNROLL = 8
SCORE_BOUND = 60.0

QB_BLOCK = 256
KB_BLOCK = QB_BLOCK
LOG2E = math.log2(math.e)
DEAD_LOG2 = -151.0
NEG_BIG = -1e30

VMEM_LIMIT = 56 * 1024 * 1024

f32 = jnp.float32
bf16 = jnp.bfloat16


def _resident(shape, index_map):
    return pl.BlockSpec(shape, index_map, pipeline_mode=pl.Buffered(1))


def _nt_dot(a, b):
    return lax.dot_general(a, b, (((1,), (1,)), ((), ())), preferred_element_type=f32)


def _dot(a, b):
    return jnp.dot(a, b, preferred_element_type=f32)


def _sigmoid(x):
    return 1.0 / (1.0 + jnp.exp(-x))


def _stack_heads(q2, lane):
    return jnp.concatenate(
        [jnp.where((lane >= h * HEAD_DIM) & (lane < (h + 1) * HEAD_DIM), q2, jnp.zeros_like(q2))
         for h in range(HEADS_PER_BLOCK)], axis=0)


def _in_proj_kernel(x_ref, g_ref, w_ref, seg_ref, qg_ref, kg_ref, o_ref):
    x = x_ref[...]
    ms = jnp.mean(x * x, axis=-1, keepdims=True)
    hn = (x * lax.rsqrt(ms + EPS) * g_ref[...]).astype(bf16)
    n_chunks = IN_WIDTH // PROJ_CHUNK
    gate_first = 6 * WIDTH // PROJ_CHUNK
    for c in range(n_chunks):
        cols = slice(c * PROJ_CHUNK, (c + 1) * PROJ_CHUNK)
        acc = _dot(hn, w_ref[:, cols].astype(bf16))
        if c in (0, 1):
            ss = _dot((acc * acc).astype(bf16), seg_ref[...])
            gain = qg_ref[...] if c == 0 else kg_ref[...]
            acc = acc * lax.rsqrt(ss * (1.0 / HEAD_DIM) + EPS) * gain
        if c in (0, 3):
            acc = acc * (HEAD_DIM ** -0.5 * LOG2E)
        if c >= gate_first:
            acc = _sigmoid(acc)
        o_ref[:, cols] = acc.astype(bf16)


def _in_proj(x2d, norm_g, w_in, seg, qg, kg):
    t = x2d.shape[0]
    return pl.pallas_call(
        _in_proj_kernel,
        grid=(t // ROW_TILE,),
        in_specs=[
            pl.BlockSpec((ROW_TILE, D_MODEL), lambda i: (i, 0)),
            _resident((1, D_MODEL), lambda i: (0, 0)),
            _resident((D_MODEL, IN_WIDTH), lambda i: (0, 0)),
            _resident((WIDTH, WIDTH), lambda i: (0, 0)),
            _resident((1, WIDTH), lambda i: (0, 0)),
            _resident((1, WIDTH), lambda i: (0, 0)),
        ],
        out_specs=pl.BlockSpec((ROW_TILE, IN_WIDTH), lambda i: (i, 0)),
        out_shape=jax.ShapeDtypeStruct((t, IN_WIDTH), bf16),
        compiler_params=pltpu.CompilerParams(
            dimension_semantics=("parallel",), vmem_limit_bytes=VMEM_LIMIT),
        name="in_proj",
    )(x2d, norm_g, w_in, seg, qg, kg)


def _attn_a_kernel(bounded_ref, q_ref, k_ref, v_ref, bias_ref, o_ref, kpad, vpad):
    t = pl.program_id(2)

    @pl.when(t == 0)
    def _():
        zeros = jnp.zeros((PAD, LANES), bf16)
        kpad[0:PAD, :] = zeros
        vpad[0:PAD, :] = zeros
        kpad[PAD:, :] = k_ref[...]
        vpad[PAD:, :] = v_ref[...]

    rows = HEADS_PER_BLOCK * QA_BLOCK
    lane = lax.broadcasted_iota(jnp.int32, (QA_BLOCK, LANES), 1)
    col = lax.broadcasted_iota(jnp.int32, (rows, BAND), 1)
    blocks_per_tile = Q_TILE // QA_BLOCK

    def make_body(band_has_padding, bounded):
        def body(ii, carry):
            qi = t * blocks_per_tile + ii
            r0 = pl.multiple_of(ii * QA_BLOCK, QA_BLOCK)
            b0 = pl.multiple_of(qi * QA_BLOCK, QA_BLOCK)
            qs = _stack_heads(q_ref[pl.ds(r0, QA_BLOCK), :], lane)
            kb = kpad[pl.ds(b0, BAND), :]
            vb = vpad[pl.ds(b0, BAND), :]
            s = _nt_dot(qs, kb) + bias_ref[...].reshape(rows, BAND)
            if band_has_padding:
                s = jnp.where(col < (PAD - qi * QA_BLOCK), NEG_BIG, s)
            if bounded:
                p = jnp.exp2(s)
            else:
                p = jnp.exp2(s - jnp.max(s, axis=-1, keepdims=True))
            l = jnp.sum(p, axis=-1, keepdims=True)
            o = _dot(p.astype(bf16), vb) * (1.0 / l)
            o = jnp.where(lane < HEAD_DIM, o[:QA_BLOCK], o[QA_BLOCK:])
            o_ref[pl.ds(r0, QA_BLOCK), :] = o.astype(bf16)
            return carry
        return body

    padded_blocks = PAD // QA_BLOCK

    def sweep(bounded):
        @pl.when(t == 0)
        def _():
            lax.fori_loop(0, padded_blocks, make_body(True, bounded), 0, unroll=A_UNROLL)
            lax.fori_loop(padded_blocks, blocks_per_tile, make_body(False, bounded), 0,
                          unroll=A_UNROLL)

        @pl.when(t != 0)
        def _():
            lax.fori_loop(0, blocks_per_tile, make_body(False, bounded), 0, unroll=A_UNROLL)

    is_bounded = bounded_ref[0] != 0
    pl.when(is_bounded)(lambda: sweep(True))
    pl.when(jnp.logical_not(is_bounded))(lambda: sweep(False))


def _attn_a(proj3, bias_tab, bounded):
    b, s, _ = proj3.shape
    return pl.pallas_call(
        _attn_a_kernel,
        grid=(b, N_PAIRS, s // Q_TILE),
        in_specs=[
            pl.BlockSpec(memory_space=pltpu.SMEM),
            pl.BlockSpec((None, Q_TILE, LANES), lambda bi, p, t: (bi, t, COL_QA + p)),
            pl.BlockSpec((None, s, LANES), lambda bi, p, t: (bi, 0, COL_KA + p)),
            pl.BlockSpec((None, s, LANES), lambda bi, p, t: (bi, 0, COL_VA + p)),
            pl.BlockSpec((HEADS_PER_BLOCK, QA_BLOCK, BAND), lambda bi, p, t: (p, 0, 0)),
        ],
        out_specs=pl.BlockSpec((None, Q_TILE, LANES), lambda bi, p, t: (bi, t, p)),
        out_shape=jax.ShapeDtypeStruct((b, s, WIDTH), bf16),
        scratch_shapes=[pltpu.VMEM((PAD + s, LANES), bf16),
                        pltpu.VMEM((PAD + s, LANES), bf16)],
        compiler_params=pltpu.CompilerParams(
            dimension_semantics=("parallel", "parallel", "arbitrary"),
            vmem_limit_bytes=VMEM_LIMIT),
        name="attn_a",
    )(bounded, proj3, proj3, proj3, bias_tab)


def _neg_abs(x):
    bits = lax.bitcast_convert_type(x, jnp.uint32) | jnp.uint32(0x80000000)
    return lax.bitcast_convert_type(bits, f32)


def _attn_b_kernel(q_ref, k_ref, v_ref, o_ref, acc_ref, carry_ref):
    t = pl.program_id(2)
    rows = HEADS_PER_BLOCK * QB_BLOCK
    lane = lax.broadcasted_iota(jnp.int32, (QB_BLOCK, LANES), 1)
    row = lax.broadcasted_iota(jnp.int32, (rows, KB_BLOCK), 0)
    col = lax.broadcasted_iota(jnp.int32, (rows, KB_BLOCK), 1)
    causal = col < (row & (QB_BLOCK - 1))
    neg_tri = jnp.where(lax.broadcasted_iota(jnp.int32, (KB_BLOCK, KB_BLOCK), 0)
                        >= lax.broadcasted_iota(jnp.int32, (KB_BLOCK, KB_BLOCK), 1),
                        -1.0, 0.0).astype(bf16)
    blocks_per_tile = Q_TILE // QB_BLOCK

    def scores(qs, k0):
        z = _nt_dot(qs, k_ref[pl.ds(k0, KB_BLOCK), :])
        sp = jnp.maximum(z, 0.0) + jnp.log2(1.0 + jnp.exp2(_neg_abs(z)))
        return z, sp

    def weights(z, sp, diag):
        log_keep_from = _dot(sp.astype(bf16), neg_tri)
        w = jnp.exp2(z + log_keep_from)
        if diag:
            w = jnp.where(causal, w, 0.0)
        return w.astype(bf16), log_keep_from[:, 0:1]

    def alive_of(c):
        return (jnp.max(c) > DEAD_LOG2).astype(jnp.int32)

    def load_q(ii):
        r0 = pl.multiple_of(ii * QB_BLOCK, QB_BLOCK)
        q0 = pl.multiple_of(t * Q_TILE + ii * QB_BLOCK, QB_BLOCK)
        return q0, _stack_heads(q_ref[pl.ds(r0, QB_BLOCK), :], lane)

    def emit(ii, acc):
        r0 = pl.multiple_of(ii * QB_BLOCK, QB_BLOCK)
        o = jnp.where(lane < HEAD_DIM, acc[:QB_BLOCK], acc[QB_BLOCK:])
        o_ref[pl.ds(r0, QB_BLOCK), :] = o.astype(bf16)

    def first_pass(ii, loop_carry):
        q0, qs = load_q(ii)
        z_d, sp_d = scores(qs, q0)
        w_d, tot_d = weights(z_d, jnp.where(causal, sp_d, 0.0), True)
        has_prev = q0 > 0
        k_prev = pl.multiple_of(jnp.maximum(q0 - KB_BLOCK, 0), KB_BLOCK)
        z_p, sp_p = scores(jnp.where(has_prev, qs, jnp.zeros_like(qs)), k_prev)
        w_p, tot_p = weights(z_p, sp_p, False)
        scale_p = jnp.where(has_prev, jnp.exp2(tot_d), 0.0)
        acc = (_dot(w_d, v_ref[pl.ds(q0, KB_BLOCK), :])
               + _dot(w_p, v_ref[pl.ds(k_prev, KB_BLOCK), :]) * scale_p)
        acc_ref[ii] = acc
        carry_ref[ii] = tot_d + tot_p
        emit(ii, acc)
        return loop_carry

    lax.fori_loop(0, blocks_per_tile, first_pass, 0, unroll=8)

    def older_blocks(ii, loop_carry):
        q0, qs = load_q(ii)

        def cond(st):
            j, _, _, alive = st
            return jnp.logical_and(j >= 0, alive > 0)

        def sweep(st):
            j, c, a, _ = st
            k0 = pl.multiple_of(j * KB_BLOCK, KB_BLOCK)
            z, sp = scores(qs, k0)
            w, tot = weights(z, sp, False)
            a = a + _dot(w, v_ref[pl.ds(k0, KB_BLOCK), :]) * jnp.exp2(c)
            c = c + tot
            return j - 1, c, a, alive_of(c)

        carry = carry_ref[ii]
        _, _, acc, _ = lax.while_loop(
            cond, sweep, (q0 // KB_BLOCK - 2, carry, acc_ref[ii], alive_of(carry)))
        emit(ii, acc)
        return loop_carry

    @pl.when(jnp.max(carry_ref[...]) > DEAD_LOG2)
    def _():
        lax.fori_loop(0, blocks_per_tile, older_blocks, 0)


def _attn_b(proj3):
    b, s, _ = proj3.shape
    return pl.pallas_call(
        _attn_b_kernel,
        grid=(b, N_PAIRS, s // Q_TILE),
        in_specs=[
            pl.BlockSpec((None, Q_TILE, LANES), lambda bi, p, t: (bi, t, COL_QB + p)),
            pl.BlockSpec((None, s, LANES), lambda bi, p, t: (bi, 0, COL_KB + p)),
            pl.BlockSpec((None, s, LANES), lambda bi, p, t: (bi, 0, COL_VB + p)),
        ],
        out_specs=pl.BlockSpec((None, Q_TILE, LANES), lambda bi, p, t: (bi, t, p)),
        out_shape=jax.ShapeDtypeStruct((b, s, WIDTH), bf16),
        scratch_shapes=[
            pltpu.VMEM((Q_TILE // QB_BLOCK, HEADS_PER_BLOCK * QB_BLOCK, LANES), f32),
            pltpu.VMEM((Q_TILE // QB_BLOCK, HEADS_PER_BLOCK * QB_BLOCK, 1), f32)],
        compiler_params=pltpu.CompilerParams(
            dimension_semantics=("parallel", "parallel", "arbitrary"),
            vmem_limit_bytes=VMEM_LIMIT),
        name="attn_b",
    )(proj3, proj3, proj3)


def _mix_ffn_kernel(x_ref, oa_ref, ob_ref, ga_ref, gb_ref, wa_ref, wb_ref, wo_ref, g_ref,
                    wup_ref, cw_ref, cb_ref, wdn_ref, o_ref, tail_ref, hbuf, act_ref):
    @pl.when(pl.program_id(1) == 0)
    def _():
        tail_ref[...] = jnp.zeros_like(tail_ref)

    ya = _dot(oa_ref[...], wa_ref[...].astype(bf16))
    yb = _dot(ob_ref[...], wb_ref[...].astype(bf16))
    mixed = ga_ref[...].astype(f32) * ya + gb_ref[...].astype(f32) * yb
    x = x_ref[...] + _dot(mixed.astype(bf16), wo_ref[...].astype(bf16))
    o_ref[...] = x
    ms = jnp.mean(x * x, axis=-1, keepdims=True)
    hn = (x * lax.rsqrt(ms + EPS) * g_ref[...]).astype(bf16)
    for c in range(D_FF // FFN_CHUNK):
        halves = []
        for half in range(2):
            c0 = half * D_FF + c * FFN_CHUNK
            cols = slice(c0, c0 + FFN_CHUNK)
            h = _dot(hn, wup_ref[:, cols])
            hbuf[half, 0:SUBLANES, :] = tail_ref[:, cols]
            hbuf[half, SUBLANES:, :] = h
            tail_ref[:, cols] = h[ROW_TILE - SUBLANES:, :]
            y = cb_ref[:, cols] + cw_ref[CONV_WIDTH - 1:CONV_WIDTH, cols] * h
            for back in range(1, CONV_WIDTH):
                shifted = hbuf[half, SUBLANES - back:SUBLANES - back + ROW_TILE, :]
                y = y + cw_ref[CONV_WIDTH - 1 - back:CONV_WIDTH - back, cols] * shifted
            halves.append(y)
        gate, up = halves
        act_ref[:, c * FFN_CHUNK:(c + 1) * FFN_CHUNK] = (gate * _sigmoid(gate) * up).astype(bf16)
    o_ref[...] += _dot(act_ref[...], wdn_ref[...].astype(bf16))


def _mix_ffn(x3, oa, ob, proj3, wa, wb, wo, norm_g, w_up, conv_w, conv_b, w_down):
    b, s, _ = x3.shape
    gate_a_blk = 6 * WIDTH // D_MODEL
    row_spec = lambda w, blk: pl.BlockSpec((None, ROW_TILE, w), lambda bi, i: (bi, i, blk))
    const = lambda shape: _resident(shape, lambda bi, i: (0, 0))
    return pl.pallas_call(
        _mix_ffn_kernel,
        grid=(b, s // ROW_TILE),
        in_specs=[
            row_spec(D_MODEL, 0), row_spec(WIDTH, 0), row_spec(WIDTH, 0),
            row_spec(D_MODEL, gate_a_blk), row_spec(D_MODEL, gate_a_blk + 1),
            const((WIDTH, D_MODEL)), const((WIDTH, D_MODEL)), const((D_MODEL, D_MODEL)),
            const((1, D_MODEL)), const((D_MODEL, 2 * D_FF)), const((CONV_WIDTH, 2 * D_FF)),
            const((1, 2 * D_FF)), const((D_FF, D_MODEL)),
        ],
        out_specs=row_spec(D_MODEL, 0),
        out_shape=jax.ShapeDtypeStruct((b, s, D_MODEL), f32),
        scratch_shapes=[pltpu.VMEM((SUBLANES, 2 * D_FF), f32),
                        pltpu.VMEM((2, SUBLANES + ROW_TILE, FFN_CHUNK), f32),
                        pltpu.VMEM((ROW_TILE, D_FF), bf16)],
        compiler_params=pltpu.CompilerParams(
            dimension_semantics=("parallel", "arbitrary"), vmem_limit_bytes=VMEM_LIMIT),
        name="mix_ffn",
    )(x3, oa, ob, proj3, proj3, wa, wb, wo, norm_g, w_up, conv_w, conv_b, w_down)


def _band_bias_table(rel_bias):
    h = rel_bias.shape[0]
    period = BAND + QA_BLOCK
    far = rel_bias[:, 2 * MAX_REL:]
    by_offset = jnp.concatenate([
        jnp.broadcast_to(far, (h, PAD - MAX_REL + 1)),
        rel_bias[:, 2 * MAX_REL - 1:0:-1],
        jnp.zeros((h, 1), rel_bias.dtype),
        jnp.broadcast_to(far, (h, QA_BLOCK - 1)),
    ], axis=1).astype(f32)
    tab = jnp.tile(by_offset, (1, QA_BLOCK))[:, :QA_BLOCK * (period - 1)]
    tab = tab.reshape(h, QA_BLOCK, period - 1)[:, :, :BAND]
    q_chunk = np.arange(QA_BLOCK)[:, None] // CHUNK
    k_chunk = np.floor_divide(np.arange(BAND)[None, :] - PAD, CHUNK)
    in_band = (k_chunk <= q_chunk) & (k_chunk >= q_chunk - LEFT_CHUNKS)
    return jnp.asarray(in_band)[None], tab * LOG2E


def _score_bound(q_gain, k_gain, rel_bias):
    dot_bound = HEAD_DIM ** 0.5 * jnp.max(jnp.abs(q_gain)) * jnp.max(jnp.abs(k_gain))
    return LOG2E * (1.02 * dot_bound + jnp.max(jnp.abs(rel_bias)))


def kernel(x, norm1_g, w_in, q_norm_g, k_norm_g, rel_bias, w_branch_a, w_branch_b, w_out,
           norm2_g, w_ffn_up, ffn_conv_w, ffn_conv_b, w_ffn_down):
    b, s, d = x.shape
    depth = w_in.shape[0]
    head_of = np.arange(WIDTH) // HEAD_DIM
    seg = jnp.asarray(head_of[:, None] == head_of[None, :], dtype=bf16)
    for l in range(depth):
        qg = jnp.tile(q_norm_g[l].astype(f32), N_HEADS)[None, :]
        kg = jnp.tile(k_norm_g[l].astype(f32), N_HEADS)[None, :]
        proj = _in_proj(x.reshape(b * s, d), norm1_g[l][None, :], w_in[l], seg, qg, kg)
        proj3 = proj.reshape(b, s, IN_WIDTH)
        bound = _score_bound(q_norm_g[l].astype(f32), k_norm_g[l].astype(f32), rel_bias[l].astype(f32))
        bounded = bound <= SCORE_BOUND
        in_band, tab = _band_bias_table(rel_bias[l])
        tab = jnp.where(in_band, tab - jnp.where(bounded, bound, 0.0), NEG_BIG)
        out_a = _attn_a(proj3, tab, bounded.astype(jnp.int32).reshape(1))
        out_b = _attn_b(proj3)
        x = _mix_ffn(x, out_a, out_b, proj3, w_branch_a[l], w_branch_b[l], w_out[l],
                     norm2_g[l][None, :], w_ffn_up[l].astype(bf16),
                     ffn_conv_w[l], ffn_conv_b[l][None, :], w_ffn_down[l])
    return x
```

```python
import math

import jax
import jax.numpy as jnp
import numpy as np
from jax import lax
from jax.experimental import pallas as pl
from jax.experimental.pallas import tpu as pltpu

D_MODEL = 1024
CHUNK = 64
LEFT_CHUNKS = 8
HEAD_DIM = 64
N_HEADS = 8
WIDTH = N_HEADS * HEAD_DIM
MAX_REL = 128
D_FF = 2816
CONV_WIDTH = 3
EPS = 1e-6

LANES = 128
SUBLANES = 8
MXU_DIM = 256
HEADS_PER_BLOCK = LANES // HEAD_DIM
N_PAIRS = N_HEADS // HEADS_PER_BLOCK
IN_WIDTH = 6 * WIDTH + 2 * D_MODEL

COL_QA, COL_KA, COL_VA = 0, WIDTH // LANES, 2 * WIDTH // LANES
COL_QB, COL_KB, COL_VB = 3 * WIDTH // LANES, 4 * WIDTH // LANES, 5 * WIDTH // LANES

ROW_TILE = 512
PROJ_CHUNK = 512
FFN_CHUNK = 256

QA_BLOCK = 128
BAND = LEFT_CHUNKS * CHUNK + QA_BLOCK
PAD = LEFT_CHUNKS * CHUNK
Q_TILE = 2048
A_UNROLL = 8
SCORE_BOUND = 60.0

QB_BLOCK = 256
KB_BLOCK = QB_BLOCK
LOG2E = math.log2(math.e)
DEAD_LOG2 = -151.0
NEG_BIG = -1e30

VMEM_LIMIT = 56 * 1024 * 1024

f32 = jnp.float32
bf16 = jnp.bfloat16


def _resident(shape, index_map):
    return pl.BlockSpec(shape, index_map, pipeline_mode=pl.Buffered(1))


def _nt_dot(a, b):
    return lax.dot_general(a, b, (((1,), (1,)), ((), ())), preferred_element_type=f32)


def _dot(a, b):
    return jnp.dot(a, b, preferred_element_type=f32)


def _sigmoid(x):
    return 1.0 / (1.0 + jnp.exp(-x))


def _stack_heads(q2, lane):
    return jnp.concatenate(
        [jnp.where((lane >= h * HEAD_DIM) & (lane < (h + 1) * HEAD_DIM), q2, jnp.zeros_like(q2))
         for h in range(HEADS_PER_BLOCK)], axis=0)


def _in_proj_kernel(x_ref, g_ref, w_ref, seg_ref, qg_ref, kg_ref, o_ref):
    x = x_ref[...]
    ms = jnp.mean(x * x, axis=-1, keepdims=True)
    hn = (x * lax.rsqrt(ms + EPS) * g_ref[...]).astype(bf16)
    n_chunks = IN_WIDTH // PROJ_CHUNK
    gate_first = 6 * WIDTH // PROJ_CHUNK
    for c in range(n_chunks):
        cols = slice(c * PROJ_CHUNK, (c + 1) * PROJ_CHUNK)
        acc = _dot(hn, w_ref[:, cols].astype(bf16))
        if c in (0, 1):
            sq = (acc * acc).astype(bf16)
            ss = jnp.concatenate(
                [_dot(sq[:, j:j + MXU_DIM], seg_ref[...]) for j in range(0, PROJ_CHUNK, MXU_DIM)],
                axis=1)
            gain = qg_ref[...] if c == 0 else kg_ref[...]
            acc = acc * lax.rsqrt(ss * (1.0 / HEAD_DIM) + EPS) * gain
        if c in (0, 3):
            acc = acc * (HEAD_DIM ** -0.5 * LOG2E)
        if c >= gate_first:
            acc = _sigmoid(acc)
        o_ref[:, cols] = acc.astype(bf16)


def _in_proj(x2d, norm_g, w_in, seg, qg, kg):
    t = x2d.shape[0]
    return pl.pallas_call(
        _in_proj_kernel,
        grid=(t // ROW_TILE,),
        in_specs=[
            pl.BlockSpec((ROW_TILE, D_MODEL), lambda i: (i, 0)),
            _resident((1, D_MODEL), lambda i: (0, 0)),
            _resident((D_MODEL, IN_WIDTH), lambda i: (0, 0)),
            _resident((MXU_DIM, MXU_DIM), lambda i: (0, 0)),
            _resident((1, WIDTH), lambda i: (0, 0)),
            _resident((1, WIDTH), lambda i: (0, 0)),
        ],
        out_specs=pl.BlockSpec((ROW_TILE, IN_WIDTH), lambda i: (i, 0)),
        out_shape=jax.ShapeDtypeStruct((t, IN_WIDTH), bf16),
        compiler_params=pltpu.CompilerParams(
            dimension_semantics=("parallel",), vmem_limit_bytes=VMEM_LIMIT),
        name="in_proj",
    )(x2d, norm_g, w_in, seg, qg, kg)


def _attn_a_kernel(bounded_ref, q_ref, k_ref, v_ref, bias_ref, o_ref, kpad, vpad):
    t = pl.program_id(2)

    @pl.when(t == 0)
    def _():
        zeros = jnp.zeros((PAD, LANES), bf16)
        kpad[0:PAD, :] = zeros
        vpad[0:PAD, :] = zeros
        kpad[PAD:, :] = k_ref[...]
        vpad[PAD:, :] = v_ref[...]

    rows = HEADS_PER_BLOCK * QA_BLOCK
    lane = lax.broadcasted_iota(jnp.int32, (QA_BLOCK, LANES), 1)
    col = lax.broadcasted_iota(jnp.int32, (rows, BAND), 1)
    blocks_per_tile = Q_TILE // QA_BLOCK

    def make_body(band_has_padding, bounded):
        def body(ii, carry):
            qi = t * blocks_per_tile + ii
            r0 = pl.multiple_of(ii * QA_BLOCK, QA_BLOCK)
            b0 = pl.multiple_of(qi * QA_BLOCK, QA_BLOCK)
            qs = _stack_heads(q_ref[pl.ds(r0, QA_BLOCK), :], lane)
            kb = kpad[pl.ds(b0, BAND), :]
            vb = vpad[pl.ds(b0, BAND), :]
            s = _nt_dot(qs, kb) + bias_ref[...].reshape(rows, BAND)
            if band_has_padding:
                s = jnp.where(col < (PAD - qi * QA_BLOCK), NEG_BIG, s)
            if bounded:
                p = jnp.exp2(s)
            else:
                p = jnp.exp2(s - jnp.max(s, axis=-1, keepdims=True))
            l = jnp.sum(p, axis=-1, keepdims=True)
            o = _dot(p.astype(bf16), vb) * (1.0 / l)
            o = jnp.where(lane < HEAD_DIM, o[:QA_BLOCK], o[QA_BLOCK:])
            o_ref[pl.ds(r0, QA_BLOCK), :] = o.astype(bf16)
            return carry
        return body

    padded_blocks = PAD // QA_BLOCK

    def sweep(bounded):
        @pl.when(t == 0)
        def _():
            lax.fori_loop(0, padded_blocks, make_body(True, bounded), 0, unroll=A_UNROLL)
            lax.fori_loop(padded_blocks, blocks_per_tile, make_body(False, bounded), 0,
                          unroll=A_UNROLL)

        @pl.when(t != 0)
        def _():
            lax.fori_loop(0, blocks_per_tile, make_body(False, bounded), 0, unroll=A_UNROLL)

    is_bounded = bounded_ref[0] != 0
    pl.when(is_bounded)(lambda: sweep(True))
    pl.when(jnp.logical_not(is_bounded))(lambda: sweep(False))


def _attn_a(proj3, bias_tab, bounded):
    b, s, _ = proj3.shape
    return pl.pallas_call(
        _attn_a_kernel,
        grid=(b, N_PAIRS, s // Q_TILE),
        in_specs=[
            pl.BlockSpec(memory_space=pltpu.SMEM),
            pl.BlockSpec((None, Q_TILE, LANES), lambda bi, p, t: (bi, t, COL_QA + p)),
            pl.BlockSpec((None, s, LANES), lambda bi, p, t: (bi, 0, COL_KA + p)),
            pl.BlockSpec((None, s, LANES), lambda bi, p, t: (bi, 0, COL_VA + p)),
            pl.BlockSpec((HEADS_PER_BLOCK, QA_BLOCK, BAND), lambda bi, p, t: (p, 0, 0)),
        ],
        out_specs=pl.BlockSpec((None, Q_TILE, LANES), lambda bi, p, t: (bi, t, p)),
        out_shape=jax.ShapeDtypeStruct((b, s, WIDTH), bf16),
        scratch_shapes=[pltpu.VMEM((PAD + s, LANES), bf16),
                        pltpu.VMEM((PAD + s, LANES), bf16)],
        compiler_params=pltpu.CompilerParams(
            dimension_semantics=("parallel", "parallel", "arbitrary"),
            vmem_limit_bytes=VMEM_LIMIT),
        name="attn_a",
    )(bounded, proj3, proj3, proj3, bias_tab)


def _neg_abs(x):
    bits = lax.bitcast_convert_type(x, jnp.uint32) | jnp.uint32(0x80000000)
    return lax.bitcast_convert_type(bits, f32)


def _attn_b_kernel(q_ref, k_ref, v_ref, o_ref, acc_ref, carry_ref):
    t = pl.program_id(2)
    rows = HEADS_PER_BLOCK * QB_BLOCK
    lane = lax.broadcasted_iota(jnp.int32, (QB_BLOCK, LANES), 1)
    row = lax.broadcasted_iota(jnp.int32, (rows, KB_BLOCK), 0)
    col = lax.broadcasted_iota(jnp.int32, (rows, KB_BLOCK), 1)
    causal = col < (row & (QB_BLOCK - 1))
    neg_tri = jnp.where(lax.broadcasted_iota(jnp.int32, (KB_BLOCK, KB_BLOCK), 0)
                        >= lax.broadcasted_iota(jnp.int32, (KB_BLOCK, KB_BLOCK), 1),
                        -1.0, 0.0).astype(bf16)
    blocks_per_tile = Q_TILE // QB_BLOCK

    def scores(qs, k0):
        z = _nt_dot(qs, k_ref[pl.ds(k0, KB_BLOCK), :])
        sp = jnp.maximum(z, 0.0) + jnp.log2(1.0 + jnp.exp2(_neg_abs(z)))
        return z, sp

    def weights(z, sp, diag):
        log_keep_from = _dot(sp.astype(bf16), neg_tri)
        w = jnp.exp2(z + log_keep_from)
        if diag:
            w = jnp.where(causal, w, 0.0)
        return w.astype(bf16), log_keep_from[:, 0:1]

    def alive_of(c):
        return (jnp.max(c) > DEAD_LOG2).astype(jnp.int32)

    def load_q(ii):
        r0 = pl.multiple_of(ii * QB_BLOCK, QB_BLOCK)
        q0 = pl.multiple_of(t * Q_TILE + ii * QB_BLOCK, QB_BLOCK)
        return q0, _stack_heads(q_ref[pl.ds(r0, QB_BLOCK), :], lane)

    def emit(ii, acc):
        r0 = pl.multiple_of(ii * QB_BLOCK, QB_BLOCK)
        o = jnp.where(lane < HEAD_DIM, acc[:QB_BLOCK], acc[QB_BLOCK:])
        o_ref[pl.ds(r0, QB_BLOCK), :] = o.astype(bf16)

    def first_pass(ii, loop_carry):
        q0, qs = load_q(ii)
        z_d, sp_d = scores(qs, q0)
        w_d, tot_d = weights(z_d, jnp.where(causal, sp_d, 0.0), True)
        has_prev = q0 > 0
        k_prev = pl.multiple_of(jnp.maximum(q0 - KB_BLOCK, 0), KB_BLOCK)
        z_p, sp_p = scores(jnp.where(has_prev, qs, jnp.zeros_like(qs)), k_prev)
        w_p, tot_p = weights(z_p, sp_p, False)
        scale_p = jnp.where(has_prev, jnp.exp2(tot_d), 0.0)
        acc = (_dot(w_d, v_ref[pl.ds(q0, KB_BLOCK), :])
               + _dot(w_p, v_ref[pl.ds(k_prev, KB_BLOCK), :]) * scale_p)
        acc_ref[ii] = acc
        carry_ref[ii] = tot_d + tot_p
        emit(ii, acc)
        return loop_carry

    lax.fori_loop(0, blocks_per_tile, first_pass, 0, unroll=8)

    def older_blocks(ii, loop_carry):
        q0, qs = load_q(ii)

        def cond(st):
            j, _, _, alive = st
            return jnp.logical_and(j >= 0, alive > 0)

        def sweep(st):
            j, c, a, _ = st
            k0 = pl.multiple_of(j * KB_BLOCK, KB_BLOCK)
            z, sp = scores(qs, k0)
            w, tot = weights(z, sp, False)
            a = a + _dot(w, v_ref[pl.ds(k0, KB_BLOCK), :]) * jnp.exp2(c)
            c = c + tot
            return j - 1, c, a, alive_of(c)

        carry = carry_ref[ii]
        _, _, acc, _ = lax.while_loop(
            cond, sweep, (q0 // KB_BLOCK - 2, carry, acc_ref[ii], alive_of(carry)))
        emit(ii, acc)
        return loop_carry

    @pl.when(jnp.max(carry_ref[...]) > DEAD_LOG2)
    def _():
        lax.fori_loop(0, blocks_per_tile, older_blocks, 0)


def _attn_b(proj3):
    b, s, _ = proj3.shape
    return pl.pallas_call(
        _attn_b_kernel,
        grid=(b, N_PAIRS, s // Q_TILE),
        in_specs=[
            pl.BlockSpec((None, Q_TILE, LANES), lambda bi, p, t: (bi, t, COL_QB + p)),
            pl.BlockSpec((None, s, LANES), lambda bi, p, t: (bi, 0, COL_KB + p)),
            pl.BlockSpec((None, s, LANES), lambda bi, p, t: (bi, 0, COL_VB + p)),
        ],
        out_specs=pl.BlockSpec((None, Q_TILE, LANES), lambda bi, p, t: (bi, t, p)),
        out_shape=jax.ShapeDtypeStruct((b, s, WIDTH), bf16),
        scratch_shapes=[
            pltpu.VMEM((Q_TILE // QB_BLOCK, HEADS_PER_BLOCK * QB_BLOCK, LANES), f32),
            pltpu.VMEM((Q_TILE // QB_BLOCK, HEADS_PER_BLOCK * QB_BLOCK, 1), f32)],
        compiler_params=pltpu.CompilerParams(
            dimension_semantics=("parallel", "parallel", "arbitrary"),
            vmem_limit_bytes=VMEM_LIMIT),
        name="attn_b",
    )(proj3, proj3, proj3)


def _mix_ffn_kernel(x_ref, oa_ref, ob_ref, ga_ref, gb_ref, wa_ref, wb_ref, wo_ref, g_ref,
                    wup_ref, cw_ref, cb_ref, wdn_ref, o_ref, tail_ref, hbuf, act_ref):
    @pl.when(pl.program_id(1) == 0)
    def _():
        tail_ref[...] = jnp.zeros_like(tail_ref)

    ya = _dot(oa_ref[...], wa_ref[...].astype(bf16))
    yb = _dot(ob_ref[...], wb_ref[...].astype(bf16))
    mixed = ga_ref[...].astype(f32) * ya + gb_ref[...].astype(f32) * yb
    x = x_ref[...] + _dot(mixed.astype(bf16), wo_ref[...].astype(bf16))
    o_ref[...] = x
    ms = jnp.mean(x * x, axis=-1, keepdims=True)
    hn = (x * lax.rsqrt(ms + EPS) * g_ref[...]).astype(bf16)
    for c in range(D_FF // FFN_CHUNK):
        halves = []
        for half in range(2):
            c0 = half * D_FF + c * FFN_CHUNK
            cols = slice(c0, c0 + FFN_CHUNK)
            h = _dot(hn, wup_ref[:, cols])
            hbuf[half, 0:SUBLANES, :] = tail_ref[:, cols]
            hbuf[half, SUBLANES:, :] = h
            tail_ref[:, cols] = h[ROW_TILE - SUBLANES:, :]
            y = cb_ref[:, cols] + cw_ref[CONV_WIDTH - 1:CONV_WIDTH, cols] * h
            for back in range(1, CONV_WIDTH):
                shifted = hbuf[half, SUBLANES - back:SUBLANES - back + ROW_TILE, :]
                y = y + cw_ref[CONV_WIDTH - 1 - back:CONV_WIDTH - back, cols] * shifted
            halves.append(y)
        gate, up = halves
        act_ref[:, c * FFN_CHUNK:(c + 1) * FFN_CHUNK] = (gate * _sigmoid(gate) * up).astype(bf16)
    o_ref[...] += _dot(act_ref[...], wdn_ref[...].astype(bf16))


def _mix_ffn(x3, oa, ob, proj3, wa, wb, wo, norm_g, w_up, conv_w, conv_b, w_down):
    b, s, _ = x3.shape
    gate_a_blk = 6 * WIDTH // D_MODEL
    row_spec = lambda w, blk: pl.BlockSpec((None, ROW_TILE, w), lambda bi, i: (bi, i, blk))
    const = lambda shape: _resident(shape, lambda bi, i: (0, 0))
    return pl.pallas_call(
        _mix_ffn_kernel,
        grid=(b, s // ROW_TILE),
        in_specs=[
            row_spec(D_MODEL, 0), row_spec(WIDTH, 0), row_spec(WIDTH, 0),
            row_spec(D_MODEL, gate_a_blk), row_spec(D_MODEL, gate_a_blk + 1),
            const((WIDTH, D_MODEL)), const((WIDTH, D_MODEL)), const((D_MODEL, D_MODEL)),
            const((1, D_MODEL)), const((D_MODEL, 2 * D_FF)), const((CONV_WIDTH, 2 * D_FF)),
            const((1, 2 * D_FF)), const((D_FF, D_MODEL)),
        ],
        out_specs=row_spec(D_MODEL, 0),
        out_shape=jax.ShapeDtypeStruct((b, s, D_MODEL), f32),
        scratch_shapes=[pltpu.VMEM((SUBLANES, 2 * D_FF), f32),
                        pltpu.VMEM((2, SUBLANES + ROW_TILE, FFN_CHUNK), f32),
                        pltpu.VMEM((ROW_TILE, D_FF), bf16)],
        compiler_params=pltpu.CompilerParams(
            dimension_semantics=("parallel", "arbitrary"), vmem_limit_bytes=VMEM_LIMIT),
        name="mix_ffn",
    )(x3, oa, ob, proj3, proj3, wa, wb, wo, norm_g, w_up, conv_w, conv_b, w_down)


def _band_bias_table(rel_bias):
    h = rel_bias.shape[0]
    period = BAND + QA_BLOCK
    far = rel_bias[:, 2 * MAX_REL:]
    by_offset = jnp.concatenate([
        jnp.broadcast_to(far, (h, PAD - MAX_REL + 1)),
        rel_bias[:, 2 * MAX_REL - 1:0:-1],
        jnp.zeros((h, 1), rel_bias.dtype),
        jnp.broadcast_to(far, (h, QA_BLOCK - 1)),
    ], axis=1).astype(f32)
    tab = jnp.tile(by_offset, (1, QA_BLOCK))[:, :QA_BLOCK * (period - 1)]
    tab = tab.reshape(h, QA_BLOCK, period - 1)[:, :, :BAND]
    q_chunk = np.arange(QA_BLOCK)[:, None] // CHUNK
    k_chunk = np.floor_divide(np.arange(BAND)[None, :] - PAD, CHUNK)
    in_band = (k_chunk <= q_chunk) & (k_chunk >= q_chunk - LEFT_CHUNKS)
    return jnp.asarray(in_band)[None], tab * LOG2E


def _score_bound(q_gain, k_gain, rel_bias):
    dot_bound = HEAD_DIM ** 0.5 * jnp.max(jnp.abs(q_gain)) * jnp.max(jnp.abs(k_gain))
    return LOG2E * (1.02 * dot_bound + jnp.max(jnp.abs(rel_bias)))


def kernel(x, norm1_g, w_in, q_norm_g, k_norm_g, rel_bias, w_branch_a, w_branch_b, w_out,
           norm2_g, w_ffn_up, ffn_conv_w, ffn_conv_b, w_ffn_down):
    b, s, d = x.shape
    depth = w_in.shape[0]
    head_of = np.arange(MXU_DIM) // HEAD_DIM
    seg = jnp.asarray(head_of[:, None] == head_of[None, :], dtype=bf16)
    for l in range(depth):
        qg = jnp.tile(q_norm_g[l].astype(f32), N_HEADS)[None, :]
        kg = jnp.tile(k_norm_g[l].astype(f32), N_HEADS)[None, :]
        proj = _in_proj(x.reshape(b * s, d), norm1_g[l][None, :], w_in[l], seg, qg, kg)
        proj3 = proj.reshape(b, s, IN_WIDTH)
        bound = _score_bound(q_norm_g[l].astype(f32), k_norm_g[l].astype(f32), rel_bias[l].astype(f32))
        bounded = bound <= SCORE_BOUND
        in_band, tab = _band_bias_table(rel_bias[l])
        tab = jnp.where(in_band, tab - jnp.where(bounded, bound, 0.0), NEG_BIG)
        out_a = _attn_a(proj3, tab, bounded.astype(jnp.int32).reshape(1))
        out_b = _attn_b(proj3)
        x = _mix_ffn(x, out_a, out_b, proj3, w_branch_a[l], w_branch_b[l], w_out[l],
                     norm2_g[l][None, :], w_ffn_up[l].astype(bf16),
                     ffn_conv_w[l], ffn_conv_b[l][None, :], w_ffn_down[l])
    return x
```

```python
import math

import jax
import jax.numpy as jnp
import numpy as np
from jax import lax
from jax.experimental import pallas as pl
from jax.experimental.pallas import tpu as pltpu

D_MODEL = 1024
CHUNK = 64
LEFT_CHUNKS = 8
HEAD_DIM = 64
N_HEADS = 8
WIDTH = N_HEADS * HEAD_DIM
MAX_REL = 128
D_FF = 2816
CONV_WIDTH = 3
EPS = 1e-6

LANES = 128
SUBLANES = 8
MXU_DIM = 256
HEADS_PER_BLOCK = LANES // HEAD_DIM
N_PAIRS = N_HEADS // HEADS_PER_BLOCK
IN_WIDTH = 6 * WIDTH + 2 * D_MODEL

COL_QA, COL_KA, COL_VA = 0, WIDTH // LANES, 2 * WIDTH // LANES
COL_QB, COL_KB, COL_VB = 3 * WIDTH // LANES, 4 * WIDTH // LANES, 5 * WIDTH // LANES

ROW_TILE = 512
PROJ_CHUNK = 512
FFN_CHUNK = 256

QA_BLOCK = 128
BAND = LEFT_CHUNKS * CHUNK + QA_BLOCK
PAD = LEFT_CHUNKS * CHUNK
Q_TILE = 2048
A_UNROLL = 16
SCORE_BOUND = 60.0

QB_BLOCK = 256
KB_BLOCK = QB_BLOCK
LOG2E = math.log2(math.e)
DEAD_LOG2 = -151.0
NEG_BIG = -1e30

VMEM_LIMIT = 56 * 1024 * 1024

f32 = jnp.float32
bf16 = jnp.bfloat16


def _resident(shape, index_map):
    return pl.BlockSpec(shape, index_map, pipeline_mode=pl.Buffered(1))


def _nt_dot(a, b):
    return lax.dot_general(a, b, (((1,), (1,)), ((), ())), preferred_element_type=f32)


def _dot(a, b):
    return jnp.dot(a, b, preferred_element_type=f32)


def _sigmoid(x):
    return 1.0 / (1.0 + jnp.exp(-x))


def _stack_heads(q2, lane):
    return jnp.concatenate(
        [jnp.where((lane >= h * HEAD_DIM) & (lane < (h + 1) * HEAD_DIM), q2, jnp.zeros_like(q2))
         for h in range(HEADS_PER_BLOCK)], axis=0)


def _in_proj_kernel(x_ref, g_ref, w_ref, seg_ref, qg_ref, kg_ref, o_ref):
    x = x_ref[...]
    ms = jnp.mean(x * x, axis=-1, keepdims=True)
    hn = (x * lax.rsqrt(ms + EPS) * g_ref[...]).astype(bf16)
    n_chunks = IN_WIDTH // PROJ_CHUNK
    gate_first = 6 * WIDTH // PROJ_CHUNK
    order = list(range(gate_first, n_chunks)) + [0, 1, 3, 2, 4, 5]
    for c in order:
        cols = slice(c * PROJ_CHUNK, (c + 1) * PROJ_CHUNK)
        acc = _dot(hn, w_ref[:, cols].astype(bf16))
        if c in (0, 1):
            sq = (acc * acc).astype(bf16)
            ss = jnp.concatenate(
                [_dot(sq[:, j:j + MXU_DIM], seg_ref[...]) for j in range(0, PROJ_CHUNK, MXU_DIM)],
                axis=1)
            gain = qg_ref[...] if c == 0 else kg_ref[...]
            acc = acc * lax.rsqrt(ss * (1.0 / HEAD_DIM) + EPS) * gain
        if c in (0, 3):
            acc = acc * (HEAD_DIM ** -0.5 * LOG2E)
        if c >= gate_first:
            acc = _sigmoid(acc)
        o_ref[:, cols] = acc.astype(bf16)


def _in_proj(x2d, norm_g, w_in, seg, qg, kg):
    t = x2d.shape[0]
    return pl.pallas_call(
        _in_proj_kernel,
        grid=(t // ROW_TILE,),
        in_specs=[
            pl.BlockSpec((ROW_TILE, D_MODEL), lambda i: (i, 0)),
            _resident((1, D_MODEL), lambda i: (0, 0)),
            _resident((D_MODEL, IN_WIDTH), lambda i: (0, 0)),
            _resident((MXU_DIM, MXU_DIM), lambda i: (0, 0)),
            _resident((1, WIDTH), lambda i: (0, 0)),
            _resident((1, WIDTH), lambda i: (0, 0)),
        ],
        out_specs=pl.BlockSpec((ROW_TILE, IN_WIDTH), lambda i: (i, 0)),
        out_shape=jax.ShapeDtypeStruct((t, IN_WIDTH), bf16),
        compiler_params=pltpu.CompilerParams(
            dimension_semantics=("parallel",), vmem_limit_bytes=VMEM_LIMIT),
        name="in_proj",
    )(x2d, norm_g, w_in, seg, qg, kg)


def _attn_a_kernel(bounded_ref, q_ref, k_ref, v_ref, bias_ref, o_ref, kpad, vpad):
    t = pl.program_id(2)

    @pl.when(t == 0)
    def _():
        zeros = jnp.zeros((PAD, LANES), bf16)
        kpad[0:PAD, :] = zeros
        vpad[0:PAD, :] = zeros
        kpad[PAD:, :] = k_ref[...]
        vpad[PAD:, :] = v_ref[...]

    rows = HEADS_PER_BLOCK * QA_BLOCK
    lane = lax.broadcasted_iota(jnp.int32, (QA_BLOCK, LANES), 1)
    col = lax.broadcasted_iota(jnp.int32, (rows, BAND), 1)
    blocks_per_tile = Q_TILE // QA_BLOCK

    def make_body(band_has_padding, bounded):
        def body(ii, carry):
            qi = t * blocks_per_tile + ii
            r0 = pl.multiple_of(ii * QA_BLOCK, QA_BLOCK)
            b0 = pl.multiple_of(qi * QA_BLOCK, QA_BLOCK)
            qs = _stack_heads(q_ref[pl.ds(r0, QA_BLOCK), :], lane)
            kb = kpad[pl.ds(b0, BAND), :]
            vb = vpad[pl.ds(b0, BAND), :]
            s = _nt_dot(qs, kb) + bias_ref[...].reshape(rows, BAND)
            if band_has_padding:
                s = jnp.where(col < (PAD - qi * QA_BLOCK), NEG_BIG, s)
            if bounded:
                p = jnp.exp2(s)
            else:
                p = jnp.exp2(s - jnp.max(s, axis=-1, keepdims=True))
            l = jnp.sum(p, axis=-1, keepdims=True)
            o = _dot(p.astype(bf16), vb) * (1.0 / l)
            o = jnp.where(lane < HEAD_DIM, o[:QA_BLOCK], o[QA_BLOCK:])
            o_ref[pl.ds(r0, QA_BLOCK), :] = o.astype(bf16)
            return carry
        return body

    padded_blocks = PAD // QA_BLOCK

    def sweep(bounded):
        @pl.when(t == 0)
        def _():
            lax.fori_loop(0, padded_blocks, make_body(True, bounded), 0, unroll=A_UNROLL)
            lax.fori_loop(padded_blocks, blocks_per_tile, make_body(False, bounded), 0,
                          unroll=A_UNROLL)

        @pl.when(t != 0)
        def _():
            lax.fori_loop(0, blocks_per_tile, make_body(False, bounded), 0, unroll=A_UNROLL)

    is_bounded = bounded_ref[0] != 0
    pl.when(is_bounded)(lambda: sweep(True))
    pl.when(jnp.logical_not(is_bounded))(lambda: sweep(False))


def _attn_a(proj3, bias_tab, bounded):
    b, s, _ = proj3.shape
    return pl.pallas_call(
        _attn_a_kernel,
        grid=(b, N_PAIRS, s // Q_TILE),
        in_specs=[
            pl.BlockSpec(memory_space=pltpu.SMEM),
            pl.BlockSpec((None, Q_TILE, LANES), lambda bi, p, t: (bi, t, COL_QA + p)),
            pl.BlockSpec((None, s, LANES), lambda bi, p, t: (bi, 0, COL_KA + p)),
            pl.BlockSpec((None, s, LANES), lambda bi, p, t: (bi, 0, COL_VA + p)),
            pl.BlockSpec((HEADS_PER_BLOCK, QA_BLOCK, BAND), lambda bi, p, t: (p, 0, 0)),
        ],
        out_specs=pl.BlockSpec((None, Q_TILE, LANES), lambda bi, p, t: (bi, t, p)),
        out_shape=jax.ShapeDtypeStruct((b, s, WIDTH), bf16),
        scratch_shapes=[pltpu.VMEM((PAD + s, LANES), bf16),
                        pltpu.VMEM((PAD + s, LANES), bf16)],
        compiler_params=pltpu.CompilerParams(
            dimension_semantics=("parallel", "parallel", "arbitrary"),
            vmem_limit_bytes=VMEM_LIMIT),
        name="attn_a",
    )(bounded, proj3, proj3, proj3, bias_tab)


def _neg_abs(x):
    bits = lax.bitcast_convert_type(x, jnp.uint32) | jnp.uint32(0x80000000)
    return lax.bitcast_convert_type(bits, f32)


def _attn_b_kernel(q_ref, k_ref, v_ref, o_ref, acc_ref, carry_ref):
    t = pl.program_id(2)
    rows = HEADS_PER_BLOCK * QB_BLOCK
    lane = lax.broadcasted_iota(jnp.int32, (QB_BLOCK, LANES), 1)
    row = lax.broadcasted_iota(jnp.int32, (rows, KB_BLOCK), 0)
    col = lax.broadcasted_iota(jnp.int32, (rows, KB_BLOCK), 1)
    causal = col < (row & (QB_BLOCK - 1))
    neg_tri = jnp.where(lax.broadcasted_iota(jnp.int32, (KB_BLOCK, KB_BLOCK), 0)
                        >= lax.broadcasted_iota(jnp.int32, (KB_BLOCK, KB_BLOCK), 1),
                        -1.0, 0.0).astype(bf16)
    blocks_per_tile = Q_TILE // QB_BLOCK

    def scores(qs, k0):
        z = _nt_dot(qs, k_ref[pl.ds(k0, KB_BLOCK), :])
        sp = jnp.maximum(z, 0.0) + jnp.log2(1.0 + jnp.exp2(_neg_abs(z)))
        return z, sp

    def weights(z, sp, diag):
        log_keep_from = _dot(sp.astype(bf16), neg_tri)
        w = jnp.exp2(z + log_keep_from)
        if diag:
            w = jnp.where(causal, w, 0.0)
        return w.astype(bf16), log_keep_from[:, 0:1]

    def alive_of(c):
        return (jnp.max(c) > DEAD_LOG2).astype(jnp.int32)

    def load_q(ii):
        r0 = pl.multiple_of(ii * QB_BLOCK, QB_BLOCK)
        q0 = pl.multiple_of(t * Q_TILE + ii * QB_BLOCK, QB_BLOCK)
        return q0, _stack_heads(q_ref[pl.ds(r0, QB_BLOCK), :], lane)

    def emit(ii, acc):
        r0 = pl.multiple_of(ii * QB_BLOCK, QB_BLOCK)
        o = jnp.where(lane < HEAD_DIM, acc[:QB_BLOCK], acc[QB_BLOCK:])
        o_ref[pl.ds(r0, QB_BLOCK), :] = o.astype(bf16)

    def first_pass(ii, loop_carry):
        q0, qs = load_q(ii)
        z_d, sp_d = scores(qs, q0)
        w_d, tot_d = weights(z_d, jnp.where(causal, sp_d, 0.0), True)
        has_prev = q0 > 0
        k_prev = pl.multiple_of(jnp.maximum(q0 - KB_BLOCK, 0), KB_BLOCK)
        z_p, sp_p = scores(jnp.where(has_prev, qs, jnp.zeros_like(qs)), k_prev)
        w_p, tot_p = weights(z_p, sp_p, False)
        scale_p = jnp.where(has_prev, jnp.exp2(tot_d), 0.0)
        acc = (_dot(w_d, v_ref[pl.ds(q0, KB_BLOCK), :])
               + _dot(w_p, v_ref[pl.ds(k_prev, KB_BLOCK), :]) * scale_p)
        acc_ref[ii] = acc
        carry_ref[ii] = tot_d + tot_p
        emit(ii, acc)
        return loop_carry

    lax.fori_loop(0, blocks_per_tile, first_pass, 0, unroll=8)

    def older_blocks(ii, loop_carry):
        q0, qs = load_q(ii)

        def cond(st):
            j, _, _, alive = st
            return jnp.logical_and(j >= 0, alive > 0)

        def sweep(st):
            j, c, a, _ = st
            k0 = pl.multiple_of(j * KB_BLOCK, KB_BLOCK)
            z, sp = scores(qs, k0)
            w, tot = weights(z, sp, False)
            a = a + _dot(w, v_ref[pl.ds(k0, KB_BLOCK), :]) * jnp.exp2(c)
            c = c + tot
            return j - 1, c, a, alive_of(c)

        carry = carry_ref[ii]
        _, _, acc, _ = lax.while_loop(
            cond, sweep, (q0 // KB_BLOCK - 2, carry, acc_ref[ii], alive_of(carry)))
        emit(ii, acc)
        return loop_carry

    @pl.when(jnp.max(carry_ref[...]) > DEAD_LOG2)
    def _():
        lax.fori_loop(0, blocks_per_tile, older_blocks, 0)


def _attn_b(proj3):
    b, s, _ = proj3.shape
    return pl.pallas_call(
        _attn_b_kernel,
        grid=(b, N_PAIRS, s // Q_TILE),
        in_specs=[
            pl.BlockSpec((None, Q_TILE, LANES), lambda bi, p, t: (bi, t, COL_QB + p)),
            pl.BlockSpec((None, s, LANES), lambda bi, p, t: (bi, 0, COL_KB + p)),
            pl.BlockSpec((None, s, LANES), lambda bi, p, t: (bi, 0, COL_VB + p)),
        ],
        out_specs=pl.BlockSpec((None, Q_TILE, LANES), lambda bi, p, t: (bi, t, p)),
        out_shape=jax.ShapeDtypeStruct((b, s, WIDTH), bf16),
        scratch_shapes=[
            pltpu.VMEM((Q_TILE // QB_BLOCK, HEADS_PER_BLOCK * QB_BLOCK, LANES), f32),
            pltpu.VMEM((Q_TILE // QB_BLOCK, HEADS_PER_BLOCK * QB_BLOCK, 1), f32)],
        compiler_params=pltpu.CompilerParams(
            dimension_semantics=("parallel", "parallel", "arbitrary"),
            vmem_limit_bytes=VMEM_LIMIT),
        name="attn_b",
    )(proj3, proj3, proj3)


def _mix_ffn_kernel(x_ref, oa_ref, ob_ref, ga_ref, gb_ref, wa_ref, wb_ref, wo_ref, g_ref,
                    wup_ref, cw_ref, cb_ref, wdn_ref, o_ref, tail_ref, hbuf, act_ref):
    @pl.when(pl.program_id(1) == 0)
    def _():
        tail_ref[...] = jnp.zeros_like(tail_ref)

    ya = _dot(oa_ref[...], wa_ref[...].astype(bf16))
    yb = _dot(ob_ref[...], wb_ref[...].astype(bf16))
    mixed = ga_ref[...].astype(f32) * ya + gb_ref[...].astype(f32) * yb
    x = x_ref[...] + _dot(mixed.astype(bf16), wo_ref[...].astype(bf16))
    o_ref[...] = x
    ms = jnp.mean(x * x, axis=-1, keepdims=True)
    hn = (x * lax.rsqrt(ms + EPS) * g_ref[...]).astype(bf16)
    for c in range(D_FF // FFN_CHUNK):
        halves = []
        for half in range(2):
            c0 = half * D_FF + c * FFN_CHUNK
            cols = slice(c0, c0 + FFN_CHUNK)
            h = _dot(hn, wup_ref[:, cols])
            hbuf[half, 0:SUBLANES, :] = tail_ref[:, cols]
            hbuf[half, SUBLANES:, :] = h
            tail_ref[:, cols] = h[ROW_TILE - SUBLANES:, :]
            y = cb_ref[:, cols] + cw_ref[CONV_WIDTH - 1:CONV_WIDTH, cols] * h
            for back in range(1, CONV_WIDTH):
                shifted = hbuf[half, SUBLANES - back:SUBLANES - back + ROW_TILE, :]
                y = y + cw_ref[CONV_WIDTH - 1 - back:CONV_WIDTH - back, cols] * shifted
            halves.append(y)
        gate, up = halves
        act_ref[:, c * FFN_CHUNK:(c + 1) * FFN_CHUNK] = (gate * _sigmoid(gate) * up).astype(bf16)
    o_ref[...] += _dot(act_ref[...], wdn_ref[...].astype(bf16))


def _mix_ffn(x3, oa, ob, proj3, wa, wb, wo, norm_g, w_up, conv_w, conv_b, w_down):
    b, s, _ = x3.shape
    gate_a_blk = 6 * WIDTH // D_MODEL
    row_spec = lambda w, blk: pl.BlockSpec((None, ROW_TILE, w), lambda bi, i: (bi, i, blk))
    const = lambda shape: _resident(shape, lambda bi, i: (0, 0))
    return pl.pallas_call(
        _mix_ffn_kernel,
        grid=(b, s // ROW_TILE),
        in_specs=[
            row_spec(D_MODEL, 0), row_spec(WIDTH, 0), row_spec(WIDTH, 0),
            row_spec(D_MODEL, gate_a_blk), row_spec(D_MODEL, gate_a_blk + 1),
            const((WIDTH, D_MODEL)), const((WIDTH, D_MODEL)), const((D_MODEL, D_MODEL)),
            const((1, D_MODEL)), const((D_MODEL, 2 * D_FF)), const((CONV_WIDTH, 2 * D_FF)),
            const((1, 2 * D_FF)), const((D_FF, D_MODEL)),
        ],
        out_specs=row_spec(D_MODEL, 0),
        out_shape=jax.ShapeDtypeStruct((b, s, D_MODEL), f32),
        scratch_shapes=[pltpu.VMEM((SUBLANES, 2 * D_FF), f32),
                        pltpu.VMEM((2, SUBLANES + ROW_TILE, FFN_CHUNK), f32),
                        pltpu.VMEM((ROW_TILE, D_FF), bf16)],
        compiler_params=pltpu.CompilerParams(
            dimension_semantics=("parallel", "arbitrary"), vmem_limit_bytes=VMEM_LIMIT),
        name="mix_ffn",
    )(x3, oa, ob, proj3, proj3, wa, wb, wo, norm_g, w_up, conv_w, conv_b, w_down)


def _band_bias_table(rel_bias):
    h = rel_bias.shape[0]
    period = BAND + QA_BLOCK
    far = rel_bias[:, 2 * MAX_REL:]
    by_offset = jnp.concatenate([
        jnp.broadcast_to(far, (h, PAD - MAX_REL + 1)),
        rel_bias[:, 2 * MAX_REL - 1:0:-1],
        jnp.zeros((h, 1), rel_bias.dtype),
        jnp.broadcast_to(far, (h, QA_BLOCK - 1)),
    ], axis=1).astype(f32)
    tab = jnp.tile(by_offset, (1, QA_BLOCK))[:, :QA_BLOCK * (period - 1)]
    tab = tab.reshape(h, QA_BLOCK, period - 1)[:, :, :BAND]
    q_chunk = np.arange(QA_BLOCK)[:, None] // CHUNK
    k_chunk = np.floor_divide(np.arange(BAND)[None, :] - PAD, CHUNK)
    in_band = (k_chunk <= q_chunk) & (k_chunk >= q_chunk - LEFT_CHUNKS)
    return jnp.asarray(in_band)[None], tab * LOG2E


def _score_bound(q_gain, k_gain, rel_bias):
    dot_bound = HEAD_DIM ** 0.5 * jnp.max(jnp.abs(q_gain)) * jnp.max(jnp.abs(k_gain))
    return LOG2E * (1.02 * dot_bound + jnp.max(jnp.abs(rel_bias)))


def kernel(x, norm1_g, w_in, q_norm_g, k_norm_g, rel_bias, w_branch_a, w_branch_b, w_out,
           norm2_g, w_ffn_up, ffn_conv_w, ffn_conv_b, w_ffn_down):
    b, s, d = x.shape
    depth = w_in.shape[0]
    head_of = np.arange(MXU_DIM) // HEAD_DIM
    seg = jnp.asarray(head_of[:, None] == head_of[None, :], dtype=bf16)
    for l in range(depth):
        qg = jnp.tile(q_norm_g[l].astype(f32), N_HEADS)[None, :]
        kg = jnp.tile(k_norm_g[l].astype(f32), N_HEADS)[None, :]
        proj = _in_proj(x.reshape(b * s, d), norm1_g[l][None, :], w_in[l], seg, qg, kg)
        proj3 = proj.reshape(b, s, IN_WIDTH)
        bound = _score_bound(q_norm_g[l].astype(f32), k_norm_g[l].astype(f32), rel_bias[l].astype(f32))
        bounded = bound <= SCORE_BOUND
        in_band, tab = _band_bias_table(rel_bias[l])
        tab = jnp.where(in_band, tab - jnp.where(bounded, bound, 0.0), NEG_BIG)
        out_a = _attn_a(proj3, tab, bounded.astype(jnp.int32).reshape(1))
        out_b = _attn_b(proj3)
        x = _mix_ffn(x, out_a, out_b, proj3, w_branch_a[l], w_branch_b[l], w_out[l],
                     norm2_g[l][None, :], w_ffn_up[l].astype(bf16),
                     ffn_conv_w[l], ffn_conv_b[l][None, :], w_ffn_down[l])
    return x
```

```python
import math

import jax
import jax.numpy as jnp
import numpy as np
from jax import lax
from jax.experimental import pallas as pl
from jax.experimental.pallas import tpu as pltpu

D_MODEL = 1024
CHUNK = 64
LEFT_CHUNKS = 8
HEAD_DIM = 64
N_HEADS = 8
WIDTH = N_HEADS * HEAD_DIM
MAX_REL = 128
D_FF = 2816
CONV_WIDTH = 3
EPS = 1e-6

LANES = 128
SUBLANES = 8
MXU_DIM = 256
HEADS_PER_BLOCK = LANES // HEAD_DIM
N_PAIRS = N_HEADS // HEADS_PER_BLOCK
IN_WIDTH = 6 * WIDTH + 2 * D_MODEL

COL_QA, COL_KA, COL_VA = 0, WIDTH // LANES, 2 * WIDTH // LANES
COL_QB, COL_KB, COL_VB = 3 * WIDTH // LANES, 4 * WIDTH // LANES, 5 * WIDTH // LANES

ROW_TILE = 512
PROJ_ROWS = 1024
PROJ_CHUNK = 512
FFN_CHUNK = 256

QA_BLOCK = 128
BAND = LEFT_CHUNKS * CHUNK + QA_BLOCK
PAD = LEFT_CHUNKS * CHUNK
Q_TILE = 2048
A_UNROLL = 16
SCORE_BOUND = 60.0

QB_BLOCK = 256
KB_BLOCK = QB_BLOCK
LOG2E = math.log2(math.e)
DEAD_LOG2 = -151.0
NEG_BIG = -1e30

VMEM_LIMIT = 56 * 1024 * 1024

f32 = jnp.float32
bf16 = jnp.bfloat16


def _resident(shape, index_map):
    return pl.BlockSpec(shape, index_map, pipeline_mode=pl.Buffered(1))


def _nt_dot(a, b):
    return lax.dot_general(a, b, (((1,), (1,)), ((), ())), preferred_element_type=f32)


def _dot(a, b):
    return jnp.dot(a, b, preferred_element_type=f32)


def _sigmoid(x):
    return 1.0 / (1.0 + jnp.exp(-x))


def _stack_heads(q2, lane):
    return jnp.concatenate(
        [jnp.where((lane >= h * HEAD_DIM) & (lane < (h + 1) * HEAD_DIM), q2, jnp.zeros_like(q2))
         for h in range(HEADS_PER_BLOCK)], axis=0)


def _in_proj_kernel(x_ref, g_ref, w_ref, seg_ref, qg_ref, kg_ref, o_ref):
    x = x_ref[...]
    ms = jnp.mean(x * x, axis=-1, keepdims=True)
    hn = (x * lax.rsqrt(ms + EPS) * g_ref[...]).astype(bf16)
    n_chunks = IN_WIDTH // PROJ_CHUNK
    gate_first = 6 * WIDTH // PROJ_CHUNK
    order = list(range(gate_first, n_chunks)) + [0, 1, 3, 2, 4, 5]
    for c in order:
        cols = slice(c * PROJ_CHUNK, (c + 1) * PROJ_CHUNK)
        acc = _dot(hn, w_ref[:, cols].astype(bf16))
        if c in (0, 1):
            sq = (acc * acc).astype(bf16)
            ss = jnp.concatenate(
                [_dot(sq[:, j:j + MXU_DIM], seg_ref[...]) for j in range(0, PROJ_CHUNK, MXU_DIM)],
                axis=1)
            gain = qg_ref[...] if c == 0 else kg_ref[...]
            acc = acc * lax.rsqrt(ss * (1.0 / HEAD_DIM) + EPS) * gain
        if c in (0, 3):
            acc = acc * (HEAD_DIM ** -0.5 * LOG2E)
        if c >= gate_first:
            acc = _sigmoid(acc)
        o_ref[:, cols] = acc.astype(bf16)


def _in_proj(x2d, norm_g, w_in, seg, qg, kg):
    t = x2d.shape[0]
    return pl.pallas_call(
        _in_proj_kernel,
        grid=(t // PROJ_ROWS,),
        in_specs=[
            pl.BlockSpec((PROJ_ROWS, D_MODEL), lambda i: (i, 0)),
            _resident((1, D_MODEL), lambda i: (0, 0)),
            _resident((D_MODEL, IN_WIDTH), lambda i: (0, 0)),
            _resident((MXU_DIM, MXU_DIM), lambda i: (0, 0)),
            _resident((1, WIDTH), lambda i: (0, 0)),
            _resident((1, WIDTH), lambda i: (0, 0)),
        ],
        out_specs=pl.BlockSpec((PROJ_ROWS, IN_WIDTH), lambda i: (i, 0)),
        out_shape=jax.ShapeDtypeStruct((t, IN_WIDTH), bf16),
        compiler_params=pltpu.CompilerParams(
            dimension_semantics=("parallel",), vmem_limit_bytes=VMEM_LIMIT),
        name="in_proj",
    )(x2d, norm_g, w_in, seg, qg, kg)


def _attn_a_kernel(bounded_ref, q_ref, k_ref, v_ref, bias_ref, o_ref, kpad, vpad):
    t = pl.program_id(2)

    @pl.when(t == 0)
    def _():
        zeros = jnp.zeros((PAD, LANES), bf16)
        kpad[0:PAD, :] = zeros
        vpad[0:PAD, :] = zeros
        kpad[PAD:, :] = k_ref[...]
        vpad[PAD:, :] = v_ref[...]

    rows = HEADS_PER_BLOCK * QA_BLOCK
    lane = lax.broadcasted_iota(jnp.int32, (QA_BLOCK, LANES), 1)
    col = lax.broadcasted_iota(jnp.int32, (rows, BAND), 1)
    blocks_per_tile = Q_TILE // QA_BLOCK

    def make_body(band_has_padding, bounded):
        def body(ii, carry):
            qi = t * blocks_per_tile + ii
            r0 = pl.multiple_of(ii * QA_BLOCK, QA_BLOCK)
            b0 = pl.multiple_of(qi * QA_BLOCK, QA_BLOCK)
            qs = _stack_heads(q_ref[pl.ds(r0, QA_BLOCK), :], lane)
            kb = kpad[pl.ds(b0, BAND), :]
            vb = vpad[pl.ds(b0, BAND), :]
            s = _nt_dot(qs, kb) + bias_ref[...].reshape(rows, BAND)
            if band_has_padding:
                s = jnp.where(col < (PAD - qi * QA_BLOCK), NEG_BIG, s)
            if bounded:
                p = jnp.exp2(s)
            else:
                p = jnp.exp2(s - jnp.max(s, axis=-1, keepdims=True))
            l = jnp.sum(p, axis=-1, keepdims=True)
            o = _dot(p.astype(bf16), vb) * (1.0 / l)
            o = jnp.where(lane < HEAD_DIM, o[:QA_BLOCK], o[QA_BLOCK:])
            o_ref[pl.ds(r0, QA_BLOCK), :] = o.astype(bf16)
            return carry
        return body

    padded_blocks = PAD // QA_BLOCK

    def sweep(bounded):
        @pl.when(t == 0)
        def _():
            lax.fori_loop(0, padded_blocks, make_body(True, bounded), 0, unroll=A_UNROLL)
            lax.fori_loop(padded_blocks, blocks_per_tile, make_body(False, bounded), 0,
                          unroll=A_UNROLL)

        @pl.when(t != 0)
        def _():
            lax.fori_loop(0, blocks_per_tile, make_body(False, bounded), 0, unroll=A_UNROLL)

    is_bounded = bounded_ref[0] != 0
    pl.when(is_bounded)(lambda: sweep(True))
    pl.when(jnp.logical_not(is_bounded))(lambda: sweep(False))


def _attn_a(proj3, bias_tab, bounded):
    b, s, _ = proj3.shape
    return pl.pallas_call(
        _attn_a_kernel,
        grid=(b, N_PAIRS, s // Q_TILE),
        in_specs=[
            pl.BlockSpec(memory_space=pltpu.SMEM),
            pl.BlockSpec((None, Q_TILE, LANES), lambda bi, p, t: (bi, t, COL_QA + p)),
            pl.BlockSpec((None, s, LANES), lambda bi, p, t: (bi, 0, COL_KA + p)),
            pl.BlockSpec((None, s, LANES), lambda bi, p, t: (bi, 0, COL_VA + p)),
            pl.BlockSpec((HEADS_PER_BLOCK, QA_BLOCK, BAND), lambda bi, p, t: (p, 0, 0)),
        ],
        out_specs=pl.BlockSpec((None, Q_TILE, LANES), lambda bi, p, t: (bi, t, p)),
        out_shape=jax.ShapeDtypeStruct((b, s, WIDTH), bf16),
        scratch_shapes=[pltpu.VMEM((PAD + s, LANES), bf16),
                        pltpu.VMEM((PAD + s, LANES), bf16)],
        compiler_params=pltpu.CompilerParams(
            dimension_semantics=("parallel", "parallel", "arbitrary"),
            vmem_limit_bytes=VMEM_LIMIT),
        name="attn_a",
    )(bounded, proj3, proj3, proj3, bias_tab)


def _neg_abs(x):
    bits = lax.bitcast_convert_type(x, jnp.uint32) | jnp.uint32(0x80000000)
    return lax.bitcast_convert_type(bits, f32)


def _attn_b_kernel(q_ref, k_ref, v_ref, o_ref, acc_ref, carry_ref):
    t = pl.program_id(2)
    rows = HEADS_PER_BLOCK * QB_BLOCK
    lane = lax.broadcasted_iota(jnp.int32, (QB_BLOCK, LANES), 1)
    row = lax.broadcasted_iota(jnp.int32, (rows, KB_BLOCK), 0)
    col = lax.broadcasted_iota(jnp.int32, (rows, KB_BLOCK), 1)
    causal = col < (row & (QB_BLOCK - 1))
    neg_tri = jnp.where(lax.broadcasted_iota(jnp.int32, (KB_BLOCK, KB_BLOCK), 0)
                        >= lax.broadcasted_iota(jnp.int32, (KB_BLOCK, KB_BLOCK), 1),
                        -1.0, 0.0).astype(bf16)
    blocks_per_tile = Q_TILE // QB_BLOCK

    def scores(qs, k0):
        z = _nt_dot(qs, k_ref[pl.ds(k0, KB_BLOCK), :])
        sp = jnp.maximum(z, 0.0) + jnp.log2(1.0 + jnp.exp2(_neg_abs(z)))
        return z, sp

    def weights(z, sp, diag):
        log_keep_from = _dot(sp.astype(bf16), neg_tri)
        w = jnp.exp2(z + log_keep_from)
        if diag:
            w = jnp.where(causal, w, 0.0)
        return w.astype(bf16), log_keep_from[:, 0:1]

    def alive_of(c):
        return (jnp.max(c) > DEAD_LOG2).astype(jnp.int32)

    def load_q(ii):
        r0 = pl.multiple_of(ii * QB_BLOCK, QB_BLOCK)
        q0 = pl.multiple_of(t * Q_TILE + ii * QB_BLOCK, QB_BLOCK)
        return q0, _stack_heads(q_ref[pl.ds(r0, QB_BLOCK), :], lane)

    def emit(ii, acc):
        r0 = pl.multiple_of(ii * QB_BLOCK, QB_BLOCK)
        o = jnp.where(lane < HEAD_DIM, acc[:QB_BLOCK], acc[QB_BLOCK:])
        o_ref[pl.ds(r0, QB_BLOCK), :] = o.astype(bf16)

    def first_pass(ii, loop_carry):
        q0, qs = load_q(ii)
        z_d, sp_d = scores(qs, q0)
        w_d, tot_d = weights(z_d, jnp.where(causal, sp_d, 0.0), True)
        has_prev = q0 > 0
        k_prev = pl.multiple_of(jnp.maximum(q0 - KB_BLOCK, 0), KB_BLOCK)
        z_p, sp_p = scores(jnp.where(has_prev, qs, jnp.zeros_like(qs)), k_prev)
        w_p, tot_p = weights(z_p, sp_p, False)
        scale_p = jnp.where(has_prev, jnp.exp2(tot_d), 0.0)
        acc = (_dot(w_d, v_ref[pl.ds(q0, KB_BLOCK), :])
               + _dot(w_p, v_ref[pl.ds(k_prev, KB_BLOCK), :]) * scale_p)
        acc_ref[ii] = acc
        carry_ref[ii] = tot_d + tot_p
        emit(ii, acc)
        return loop_carry

    lax.fori_loop(0, blocks_per_tile, first_pass, 0, unroll=8)

    def older_blocks(ii, loop_carry):
        q0, qs = load_q(ii)

        def cond(st):
            j, _, _, alive = st
            return jnp.logical_and(j >= 0, alive > 0)

        def sweep(st):
            j, c, a, _ = st
            k0 = pl.multiple_of(j * KB_BLOCK, KB_BLOCK)
            z, sp = scores(qs, k0)
            w, tot = weights(z, sp, False)
            a = a + _dot(w, v_ref[pl.ds(k0, KB_BLOCK), :]) * jnp.exp2(c)
            c = c + tot
            return j - 1, c, a, alive_of(c)

        carry = carry_ref[ii]
        _, _, acc, _ = lax.while_loop(
            cond, sweep, (q0 // KB_BLOCK - 2, carry, acc_ref[ii], alive_of(carry)))
        emit(ii, acc)
        return loop_carry

    @pl.when(jnp.max(carry_ref[...]) > DEAD_LOG2)
    def _():
        lax.fori_loop(0, blocks_per_tile, older_blocks, 0)


def _attn_b(proj3):
    b, s, _ = proj3.shape
    return pl.pallas_call(
        _attn_b_kernel,
        grid=(b, N_PAIRS, s // Q_TILE),
        in_specs=[
            pl.BlockSpec((None, Q_TILE, LANES), lambda bi, p, t: (bi, t, COL_QB + p)),
            pl.BlockSpec((None, s, LANES), lambda bi, p, t: (bi, 0, COL_KB + p)),
            pl.BlockSpec((None, s, LANES), lambda bi, p, t: (bi, 0, COL_VB + p)),
        ],
        out_specs=pl.BlockSpec((None, Q_TILE, LANES), lambda bi, p, t: (bi, t, p)),
        out_shape=jax.ShapeDtypeStruct((b, s, WIDTH), bf16),
        scratch_shapes=[
            pltpu.VMEM((Q_TILE // QB_BLOCK, HEADS_PER_BLOCK * QB_BLOCK, LANES), f32),
            pltpu.VMEM((Q_TILE // QB_BLOCK, HEADS_PER_BLOCK * QB_BLOCK, 1), f32)],
        compiler_params=pltpu.CompilerParams(
            dimension_semantics=("parallel", "parallel", "arbitrary"),
            vmem_limit_bytes=VMEM_LIMIT),
        name="attn_b",
    )(proj3, proj3, proj3)


def _mix_ffn_kernel(x_ref, oa_ref, ob_ref, ga_ref, gb_ref, wa_ref, wb_ref, wo_ref, g_ref,
                    wup_ref, cw_ref, cb_ref, wdn_ref, o_ref, tail_ref, hbuf, act_ref):
    @pl.when(pl.program_id(1) == 0)
    def _():
        tail_ref[...] = jnp.zeros_like(tail_ref)

    ya = _dot(oa_ref[...], wa_ref[...].astype(bf16))
    yb = _dot(ob_ref[...], wb_ref[...].astype(bf16))
    mixed = ga_ref[...].astype(f32) * ya + gb_ref[...].astype(f32) * yb
    x = x_ref[...] + _dot(mixed.astype(bf16), wo_ref[...].astype(bf16))
    o_ref[...] = x
    ms = jnp.mean(x * x, axis=-1, keepdims=True)
    hn = (x * lax.rsqrt(ms + EPS) * g_ref[...]).astype(bf16)
    for c in range(D_FF // FFN_CHUNK):
        halves = []
        for half in range(2):
            c0 = half * D_FF + c * FFN_CHUNK
            cols = slice(c0, c0 + FFN_CHUNK)
            h = _dot(hn, wup_ref[:, cols])
            hbuf[half, 0:SUBLANES, :] = tail_ref[:, cols]
            hbuf[half, SUBLANES:, :] = h
            tail_ref[:, cols] = h[ROW_TILE - SUBLANES:, :]
            y = cb_ref[:, cols] + cw_ref[CONV_WIDTH - 1:CONV_WIDTH, cols] * h
            for back in range(1, CONV_WIDTH):
                shifted = hbuf[half, SUBLANES - back:SUBLANES - back + ROW_TILE, :]
                y = y + cw_ref[CONV_WIDTH - 1 - back:CONV_WIDTH - back, cols] * shifted
            halves.append(y)
        gate, up = halves
        act_ref[:, c * FFN_CHUNK:(c + 1) * FFN_CHUNK] = (gate * _sigmoid(gate) * up).astype(bf16)
    o_ref[...] += _dot(act_ref[...], wdn_ref[...].astype(bf16))


def _mix_ffn(x3, oa, ob, proj3, wa, wb, wo, norm_g, w_up, conv_w, conv_b, w_down):
    b, s, _ = x3.shape
    gate_a_blk = 6 * WIDTH // D_MODEL
    row_spec = lambda w, blk: pl.BlockSpec((None, ROW_TILE, w), lambda bi, i: (bi, i, blk))
    const = lambda shape: _resident(shape, lambda bi, i: (0, 0))
    return pl.pallas_call(
        _mix_ffn_kernel,
        grid=(b, s // ROW_TILE),
        in_specs=[
            row_spec(D_MODEL, 0), row_spec(WIDTH, 0), row_spec(WIDTH, 0),
            row_spec(D_MODEL, gate_a_blk), row_spec(D_MODEL, gate_a_blk + 1),
            const((WIDTH, D_MODEL)), const((WIDTH, D_MODEL)), const((D_MODEL, D_MODEL)),
            const((1, D_MODEL)), const((D_MODEL, 2 * D_FF)), const((CONV_WIDTH, 2 * D_FF)),
            const((1, 2 * D_FF)), const((D_FF, D_MODEL)),
        ],
        out_specs=row_spec(D_MODEL, 0),
        out_shape=jax.ShapeDtypeStruct((b, s, D_MODEL), f32),
        scratch_shapes=[pltpu.VMEM((SUBLANES, 2 * D_FF), f32),
                        pltpu.VMEM((2, SUBLANES + ROW_TILE, FFN_CHUNK), f32),
                        pltpu.VMEM((ROW_TILE, D_FF), bf16)],
        compiler_params=pltpu.CompilerParams(
            dimension_semantics=("parallel", "arbitrary"), vmem_limit_bytes=VMEM_LIMIT),
        name="mix_ffn",
    )(x3, oa, ob, proj3, proj3, wa, wb, wo, norm_g, w_up, conv_w, conv_b, w_down)


def _band_bias_table(rel_bias):
    h = rel_bias.shape[0]
    period = BAND + QA_BLOCK
    far = rel_bias[:, 2 * MAX_REL:]
    by_offset = jnp.concatenate([
        jnp.broadcast_to(far, (h, PAD - MAX_REL + 1)),
        rel_bias[:, 2 * MAX_REL - 1:0:-1],
        jnp.zeros((h, 1), rel_bias.dtype),
        jnp.broadcast_to(far, (h, QA_BLOCK - 1)),
    ], axis=1).astype(f32)
    tab = jnp.tile(by_offset, (1, QA_BLOCK))[:, :QA_BLOCK * (period - 1)]
    tab = tab.reshape(h, QA_BLOCK, period - 1)[:, :, :BAND]
    q_chunk = np.arange(QA_BLOCK)[:, None] // CHUNK
    k_chunk = np.floor_divide(np.arange(BAND)[None, :] - PAD, CHUNK)
    in_band = (k_chunk <= q_chunk) & (k_chunk >= q_chunk - LEFT_CHUNKS)
    return jnp.asarray(in_band)[None], tab * LOG2E


def _score_bound(q_gain, k_gain, rel_bias):
    dot_bound = HEAD_DIM ** 0.5 * jnp.max(jnp.abs(q_gain)) * jnp.max(jnp.abs(k_gain))
    return LOG2E * (1.02 * dot_bound + jnp.max(jnp.abs(rel_bias)))


def kernel(x, norm1_g, w_in, q_norm_g, k_norm_g, rel_bias, w_branch_a, w_branch_b, w_out,
           norm2_g, w_ffn_up, ffn_conv_w, ffn_conv_b, w_ffn_down):
    b, s, d = x.shape
    depth = w_in.shape[0]
    head_of = np.arange(MXU_DIM) // HEAD_DIM
    seg = jnp.asarray(head_of[:, None] == head_of[None, :], dtype=bf16)
    for l in range(depth):
        qg = jnp.tile(q_norm_g[l].astype(f32), N_HEADS)[None, :]
        kg = jnp.tile(k_norm_g[l].astype(f32), N_HEADS)[None, :]
        proj = _in_proj(x.reshape(b * s, d), norm1_g[l][None, :], w_in[l], seg, qg, kg)
        proj3 = proj.reshape(b, s, IN_WIDTH)
        bound = _score_bound(q_norm_g[l].astype(f32), k_norm_g[l].astype(f32), rel_bias[l].astype(f32))
        bounded = bound <= SCORE_BOUND
        in_band, tab = _band_bias_table(rel_bias[l])
        tab = jnp.where(in_band, tab - jnp.where(bounded, bound, 0.0), NEG_BIG)
        out_a = _attn_a(proj3, tab, bounded.astype(jnp.int32).reshape(1))
        out_b = _attn_b(proj3)
        x = _mix_ffn(x, out_a, out_b, proj3, w_branch_a[l], w_branch_b[l], w_out[l],
                     norm2_g[l][None, :], w_ffn_up[l].astype(bf16),
                     ffn_conv_w[l], ffn_conv_b[l][None, :], w_ffn_down[l])
    return x
```

```python
import math

import jax
import jax.numpy as jnp
import numpy as np
from jax import lax
from jax.experimental import pallas as pl
from jax.experimental.pallas import tpu as pltpu

D_MODEL = 1024
CHUNK = 64
LEFT_CHUNKS = 8
HEAD_DIM = 64
N_HEADS = 8
WIDTH = N_HEADS * HEAD_DIM
MAX_REL = 128
D_FF = 2816
CONV_WIDTH = 3
EPS = 1e-6

LANES = 128
SUBLANES = 8
MXU_DIM = 256
HEADS_PER_BLOCK = LANES // HEAD_DIM
N_PAIRS = N_HEADS // HEADS_PER_BLOCK
IN_WIDTH = 6 * WIDTH + 2 * D_MODEL

COL_QA, COL_KA, COL_VA = 0, WIDTH // LANES, 2 * WIDTH // LANES
COL_QB, COL_KB, COL_VB = 3 * WIDTH // LANES, 4 * WIDTH // LANES, 5 * WIDTH // LANES

ROW_TILE = 512
PROJ_ROWS = 1024
PROJ_CHUNK = 512
FFN_CHUNK = 256

QA_BLOCK = 128
BAND = LEFT_CHUNKS * CHUNK + QA_BLOCK
PAD = LEFT_CHUNKS * CHUNK
Q_TILE = 2048
A_UNROLL = 16
SCORE_BOUND = 60.0

QB_BLOCK = 256
KB_BLOCK = QB_BLOCK
LOG2E = math.log2(math.e)
DEAD_LOG2 = -151.0
NEG_BIG = -1e30

VMEM_LIMIT = 56 * 1024 * 1024

f32 = jnp.float32
bf16 = jnp.bfloat16


def _resident(shape, index_map):
    return pl.BlockSpec(shape, index_map, pipeline_mode=pl.Buffered(1))


def _nt_dot(a, b):
    return lax.dot_general(a, b, (((1,), (1,)), ((), ())), preferred_element_type=f32)


def _dot(a, b):
    return jnp.dot(a, b, preferred_element_type=f32)


def _sigmoid(x):
    return 1.0 / (1.0 + jnp.exp(-x))


def _stack_heads(q2, lane):
    return jnp.concatenate(
        [jnp.where((lane >= h * HEAD_DIM) & (lane < (h + 1) * HEAD_DIM), q2, jnp.zeros_like(q2))
         for h in range(HEADS_PER_BLOCK)], axis=0)


def _in_proj_kernel(x_ref, g_ref, w_ref, seg_ref, qg_ref, kg_ref, o_ref):
    x = x_ref[...]
    ms = jnp.mean(x * x, axis=-1, keepdims=True)
    hn = (x * lax.rsqrt(ms + EPS) * g_ref[...]).astype(bf16)
    n_chunks = IN_WIDTH // PROJ_CHUNK
    gate_first = 6 * WIDTH // PROJ_CHUNK
    order = list(range(gate_first, n_chunks)) + [0, 1, 3, 2, 4, 5]
    for c in order:
        cols = slice(c * PROJ_CHUNK, (c + 1) * PROJ_CHUNK)
        acc = _dot(hn, w_ref[:, cols].astype(bf16))
        if c in (0, 1):
            sq = (acc * acc).astype(bf16)
            ss = jnp.concatenate(
                [_dot(sq[:, j:j + MXU_DIM], seg_ref[...]) for j in range(0, PROJ_CHUNK, MXU_DIM)],
                axis=1)
            gain = qg_ref[...] if c == 0 else kg_ref[...]
            acc = acc * lax.rsqrt(ss * (1.0 / HEAD_DIM) + EPS) * gain
        if c in (0, 3):
            acc = acc * (HEAD_DIM ** -0.5 * LOG2E)
        if c >= gate_first:
            acc = _sigmoid(acc)
        o_ref[:, cols] = acc.astype(bf16)


def _in_proj(x2d, norm_g, w_in, seg, qg, kg):
    t = x2d.shape[0]
    return pl.pallas_call(
        _in_proj_kernel,
        grid=(t // PROJ_ROWS,),
        in_specs=[
            pl.BlockSpec((PROJ_ROWS, D_MODEL), lambda i: (i, 0)),
            _resident((1, D_MODEL), lambda i: (0, 0)),
            _resident((D_MODEL, IN_WIDTH), lambda i: (0, 0)),
            _resident((MXU_DIM, MXU_DIM), lambda i: (0, 0)),
            _resident((1, WIDTH), lambda i: (0, 0)),
            _resident((1, WIDTH), lambda i: (0, 0)),
        ],
        out_specs=pl.BlockSpec((PROJ_ROWS, IN_WIDTH), lambda i: (i, 0)),
        out_shape=jax.ShapeDtypeStruct((t, IN_WIDTH), bf16),
        compiler_params=pltpu.CompilerParams(
            dimension_semantics=("parallel",), vmem_limit_bytes=VMEM_LIMIT),
        name="in_proj",
    )(x2d, norm_g, w_in, seg, qg, kg)


def _attn_a_kernel(bounded_ref, q_ref, k_ref, v_ref, bias_ref, o_ref, kpad, vpad):
    t = pl.program_id(2)

    @pl.when(t == 0)
    def _():
        zeros = jnp.zeros((PAD, LANES), bf16)
        kpad[0:PAD, :] = zeros
        vpad[0:PAD, :] = zeros
        kpad[PAD:, :] = k_ref[...]
        vpad[PAD:, :] = v_ref[...]

    rows = HEADS_PER_BLOCK * QA_BLOCK
    lane = lax.broadcasted_iota(jnp.int32, (QA_BLOCK, LANES), 1)
    col = lax.broadcasted_iota(jnp.int32, (rows, BAND), 1)
    blocks_per_tile = Q_TILE // QA_BLOCK

    def make_body(band_has_padding, bounded):
        def body(ii, carry):
            qi = t * blocks_per_tile + ii
            r0 = pl.multiple_of(ii * QA_BLOCK, QA_BLOCK)
            b0 = pl.multiple_of(qi * QA_BLOCK, QA_BLOCK)
            qs = _stack_heads(q_ref[pl.ds(r0, QA_BLOCK), :], lane)
            kb = kpad[pl.ds(b0, BAND), :]
            vb = vpad[pl.ds(b0, BAND), :]
            s = _nt_dot(qs, kb) + bias_ref[...].reshape(rows, BAND)
            if band_has_padding:
                s = jnp.where(col < (PAD - qi * QA_BLOCK), NEG_BIG, s)
            if bounded:
                p = jnp.exp2(s)
            else:
                p = jnp.exp2(s - jnp.max(s, axis=-1, keepdims=True))
            l = jnp.sum(p, axis=-1, keepdims=True)
            o = _dot(p.astype(bf16), vb) * (1.0 / l)
            o = jnp.where(lane < HEAD_DIM, o[:QA_BLOCK], o[QA_BLOCK:])
            o_ref[pl.ds(r0, QA_BLOCK), :] = o.astype(bf16)
            return carry
        return body

    padded_blocks = PAD // QA_BLOCK

    def sweep(bounded):
        @pl.when(t == 0)
        def _():
            lax.fori_loop(0, padded_blocks, make_body(True, bounded), 0, unroll=A_UNROLL)
            lax.fori_loop(padded_blocks, blocks_per_tile, make_body(False, bounded), 0,
                          unroll=A_UNROLL)

        @pl.when(t != 0)
        def _():
            lax.fori_loop(0, blocks_per_tile, make_body(False, bounded), 0, unroll=A_UNROLL)

    is_bounded = bounded_ref[0] != 0
    pl.when(is_bounded)(lambda: sweep(True))
    pl.when(jnp.logical_not(is_bounded))(lambda: sweep(False))


def _attn_a(proj3, bias_tab, bounded):
    b, s, _ = proj3.shape
    return pl.pallas_call(
        _attn_a_kernel,
        grid=(b, N_PAIRS, s // Q_TILE),
        in_specs=[
            pl.BlockSpec(memory_space=pltpu.SMEM),
            pl.BlockSpec((None, Q_TILE, LANES), lambda bi, p, t: (bi, t, COL_QA + p)),
            pl.BlockSpec((None, s, LANES), lambda bi, p, t: (bi, 0, COL_KA + p)),
            pl.BlockSpec((None, s, LANES), lambda bi, p, t: (bi, 0, COL_VA + p)),
            pl.BlockSpec((HEADS_PER_BLOCK, QA_BLOCK, BAND), lambda bi, p, t: (p, 0, 0)),
        ],
        out_specs=pl.BlockSpec((None, Q_TILE, LANES), lambda bi, p, t: (bi, t, p)),
        out_shape=jax.ShapeDtypeStruct((b, s, WIDTH), bf16),
        scratch_shapes=[pltpu.VMEM((PAD + s, LANES), bf16),
                        pltpu.VMEM((PAD + s, LANES), bf16)],
        compiler_params=pltpu.CompilerParams(
            dimension_semantics=("parallel", "parallel", "arbitrary"),
            vmem_limit_bytes=VMEM_LIMIT),
        name="attn_a",
    )(bounded, proj3, proj3, proj3, bias_tab)


def _neg_abs(x):
    bits = lax.bitcast_convert_type(x, jnp.uint32) | jnp.uint32(0x80000000)
    return lax.bitcast_convert_type(bits, f32)


def _attn_b_kernel(q_ref, k_ref, v_ref, o_ref, acc_ref, carry_ref):
    t = pl.program_id(2)
    rows = HEADS_PER_BLOCK * QB_BLOCK
    lane = lax.broadcasted_iota(jnp.int32, (QB_BLOCK, LANES), 1)
    row = lax.broadcasted_iota(jnp.int32, (rows, KB_BLOCK), 0)
    col = lax.broadcasted_iota(jnp.int32, (rows, KB_BLOCK), 1)
    causal = col < (row & (QB_BLOCK - 1))
    neg_tri = jnp.where(lax.broadcasted_iota(jnp.int32, (KB_BLOCK, KB_BLOCK), 0)
                        >= lax.broadcasted_iota(jnp.int32, (KB_BLOCK, KB_BLOCK), 1),
                        -1.0, 0.0).astype(bf16)
    blocks_per_tile = Q_TILE // QB_BLOCK

    def scores(qs, k0):
        z = _nt_dot(qs, k_ref[pl.ds(k0, KB_BLOCK), :])
        sp = jnp.maximum(z, 0.0) + jnp.log2(1.0 + jnp.exp2(_neg_abs(z)))
        return z, sp

    def keep_from(sp):
        return _dot(sp.astype(bf16), neg_tri)

    def weights(z, log_keep_from, diag):
        w = jnp.exp2(z + log_keep_from)
        if diag:
            w = jnp.where(causal, w, 0.0)
        return w.astype(bf16), log_keep_from[:, 0:1]

    def alive_of(c):
        return (jnp.max(c) > DEAD_LOG2).astype(jnp.int32)

    def load_q(ii):
        r0 = pl.multiple_of(ii * QB_BLOCK, QB_BLOCK)
        q0 = pl.multiple_of(t * Q_TILE + ii * QB_BLOCK, QB_BLOCK)
        return q0, _stack_heads(q_ref[pl.ds(r0, QB_BLOCK), :], lane)

    def emit(ii, acc):
        r0 = pl.multiple_of(ii * QB_BLOCK, QB_BLOCK)
        o = jnp.where(lane < HEAD_DIM, acc[:QB_BLOCK], acc[QB_BLOCK:])
        o_ref[pl.ds(r0, QB_BLOCK), :] = o.astype(bf16)

    def first_pass(ii, loop_carry):
        q0, qs = load_q(ii)
        z_d, sp_d = scores(qs, q0)
        has_prev = q0 > 0
        k_prev = pl.multiple_of(jnp.maximum(q0 - KB_BLOCK, 0), KB_BLOCK)
        z_p, sp_p = scores(jnp.where(has_prev, qs, jnp.zeros_like(qs)), k_prev)
        lk = keep_from(jnp.concatenate([jnp.where(causal, sp_d, 0.0), sp_p], axis=0))
        w_d, tot_d = weights(z_d, lk[:rows], True)
        w_p, tot_p = weights(z_p, lk[rows:], False)
        scale_p = jnp.where(has_prev, jnp.exp2(tot_d), 0.0)
        acc = (_dot(w_d, v_ref[pl.ds(q0, KB_BLOCK), :])
               + _dot(w_p, v_ref[pl.ds(k_prev, KB_BLOCK), :]) * scale_p)
        acc_ref[ii] = acc
        carry_ref[ii] = tot_d + tot_p
        emit(ii, acc)
        return loop_carry

    lax.fori_loop(0, blocks_per_tile, first_pass, 0, unroll=8)

    def older_blocks(ii, loop_carry):
        q0, qs = load_q(ii)

        def cond(st):
            j, _, _, alive = st
            return jnp.logical_and(j >= 0, alive > 0)

        def sweep(st):
            j, c, a, _ = st
            k0 = pl.multiple_of(j * KB_BLOCK, KB_BLOCK)
            z, sp = scores(qs, k0)
            w, tot = weights(z, keep_from(sp), False)
            a = a + _dot(w, v_ref[pl.ds(k0, KB_BLOCK), :]) * jnp.exp2(c)
            c = c + tot
            return j - 1, c, a, alive_of(c)

        carry = carry_ref[ii]
        _, _, acc, _ = lax.while_loop(
            cond, sweep, (q0 // KB_BLOCK - 2, carry, acc_ref[ii], alive_of(carry)))
        emit(ii, acc)
        return loop_carry

    @pl.when(jnp.max(carry_ref[...]) > DEAD_LOG2)
    def _():
        lax.fori_loop(0, blocks_per_tile, older_blocks, 0)


def _attn_b(proj3):
    b, s, _ = proj3.shape
    return pl.pallas_call(
        _attn_b_kernel,
        grid=(b, N_PAIRS, s // Q_TILE),
        in_specs=[
            pl.BlockSpec((None, Q_TILE, LANES), lambda bi, p, t: (bi, t, COL_QB + p)),
            pl.BlockSpec((None, s, LANES), lambda bi, p, t: (bi, 0, COL_KB + p)),
            pl.BlockSpec((None, s, LANES), lambda bi, p, t: (bi, 0, COL_VB + p)),
        ],
        out_specs=pl.BlockSpec((None, Q_TILE, LANES), lambda bi, p, t: (bi, t, p)),
        out_shape=jax.ShapeDtypeStruct((b, s, WIDTH), bf16),
        scratch_shapes=[
            pltpu.VMEM((Q_TILE // QB_BLOCK, HEADS_PER_BLOCK * QB_BLOCK, LANES), f32),
            pltpu.VMEM((Q_TILE // QB_BLOCK, HEADS_PER_BLOCK * QB_BLOCK, 1), f32)],
        compiler_params=pltpu.CompilerParams(
            dimension_semantics=("parallel", "parallel", "arbitrary"),
            vmem_limit_bytes=VMEM_LIMIT),
        name="attn_b",
    )(proj3, proj3, proj3)


def _mix_ffn_kernel(x_ref, oa_ref, ob_ref, ga_ref, gb_ref, wa_ref, wb_ref, wo_ref, g_ref,
                    wup_ref, cw_ref, cb_ref, wdn_ref, o_ref, tail_ref, hbuf, act_ref):
    @pl.when(pl.program_id(1) == 0)
    def _():
        tail_ref[...] = jnp.zeros_like(tail_ref)

    ya = _dot(oa_ref[...], wa_ref[...].astype(bf16))
    yb = _dot(ob_ref[...], wb_ref[...].astype(bf16))
    mixed = ga_ref[...].astype(f32) * ya + gb_ref[...].astype(f32) * yb
    x = x_ref[...] + _dot(mixed.astype(bf16), wo_ref[...].astype(bf16))
    o_ref[...] = x
    ms = jnp.mean(x * x, axis=-1, keepdims=True)
    hn = (x * lax.rsqrt(ms + EPS) * g_ref[...]).astype(bf16)
    for c in range(D_FF // FFN_CHUNK):
        halves = []
        for half in range(2):
            c0 = half * D_FF + c * FFN_CHUNK
            cols = slice(c0, c0 + FFN_CHUNK)
            h = _dot(hn, wup_ref[:, cols])
            hbuf[half, 0:SUBLANES, :] = tail_ref[:, cols]
            hbuf[half, SUBLANES:, :] = h
            tail_ref[:, cols] = h[ROW_TILE - SUBLANES:, :]
            y = cb_ref[:, cols] + cw_ref[CONV_WIDTH - 1:CONV_WIDTH, cols] * h
            for back in range(1, CONV_WIDTH):
                shifted = hbuf[half, SUBLANES - back:SUBLANES - back + ROW_TILE, :]
                y = y + cw_ref[CONV_WIDTH - 1 - back:CONV_WIDTH - back, cols] * shifted
            halves.append(y)
        gate, up = halves
        act_ref[:, c * FFN_CHUNK:(c + 1) * FFN_CHUNK] = (gate * _sigmoid(gate) * up).astype(bf16)
    o_ref[...] += _dot(act_ref[...], wdn_ref[...].astype(bf16))


def _mix_ffn(x3, oa, ob, proj3, wa, wb, wo, norm_g, w_up, conv_w, conv_b, w_down):
    b, s, _ = x3.shape
    gate_a_blk = 6 * WIDTH // D_MODEL
    row_spec = lambda w, blk: pl.BlockSpec((None, ROW_TILE, w), lambda bi, i: (bi, i, blk))
    const = lambda shape: _resident(shape, lambda bi, i: (0, 0))
    return pl.pallas_call(
        _mix_ffn_kernel,
        grid=(b, s // ROW_TILE),
        in_specs=[
            row_spec(D_MODEL, 0), row_spec(WIDTH, 0), row_spec(WIDTH, 0),
            row_spec(D_MODEL, gate_a_blk), row_spec(D_MODEL, gate_a_blk + 1),
            const((WIDTH, D_MODEL)), const((WIDTH, D_MODEL)), const((D_MODEL, D_MODEL)),
            const((1, D_MODEL)), const((D_MODEL, 2 * D_FF)), const((CONV_WIDTH, 2 * D_FF)),
            const((1, 2 * D_FF)), const((D_FF, D_MODEL)),
        ],
        out_specs=row_spec(D_MODEL, 0),
        out_shape=jax.ShapeDtypeStruct((b, s, D_MODEL), f32),
        scratch_shapes=[pltpu.VMEM((SUBLANES, 2 * D_FF), f32),
                        pltpu.VMEM((2, SUBLANES + ROW_TILE, FFN_CHUNK), f32),
                        pltpu.VMEM((ROW_TILE, D_FF), bf16)],
        compiler_params=pltpu.CompilerParams(
            dimension_semantics=("parallel", "arbitrary"), vmem_limit_bytes=VMEM_LIMIT),
        name="mix_ffn",
    )(x3, oa, ob, proj3, proj3, wa, wb, wo, norm_g, w_up, conv_w, conv_b, w_down)


def _band_bias_table(rel_bias):
    h = rel_bias.shape[0]
    period = BAND + QA_BLOCK
    far = rel_bias[:, 2 * MAX_REL:]
    by_offset = jnp.concatenate([
        jnp.broadcast_to(far, (h, PAD - MAX_REL + 1)),
        rel_bias[:, 2 * MAX_REL - 1:0:-1],
        jnp.zeros((h, 1), rel_bias.dtype),
        jnp.broadcast_to(far, (h, QA_BLOCK - 1)),
    ], axis=1).astype(f32)
    tab = jnp.tile(by_offset, (1, QA_BLOCK))[:, :QA_BLOCK * (period - 1)]
    tab = tab.reshape(h, QA_BLOCK, period - 1)[:, :, :BAND]
    q_chunk = np.arange(QA_BLOCK)[:, None] // CHUNK
    k_chunk = np.floor_divide(np.arange(BAND)[None, :] - PAD, CHUNK)
    in_band = (k_chunk <= q_chunk) & (k_chunk >= q_chunk - LEFT_CHUNKS)
    return jnp.asarray(in_band)[None], tab * LOG2E


def _score_bound(q_gain, k_gain, rel_bias):
    dot_bound = HEAD_DIM ** 0.5 * jnp.max(jnp.abs(q_gain)) * jnp.max(jnp.abs(k_gain))
    return LOG2E * (1.02 * dot_bound + jnp.max(jnp.abs(rel_bias)))


def kernel(x, norm1_g, w_in, q_norm_g, k_norm_g, rel_bias, w_branch_a, w_branch_b, w_out,
           norm2_g, w_ffn_up, ffn_conv_w, ffn_conv_b, w_ffn_down):
    b, s, d = x.shape
    depth = w_in.shape[0]
    head_of = np.arange(MXU_DIM) // HEAD_DIM
    seg = jnp.asarray(head_of[:, None] == head_of[None, :], dtype=bf16)
    for l in range(depth):
        qg = jnp.tile(q_norm_g[l].astype(f32), N_HEADS)[None, :]
        kg = jnp.tile(k_norm_g[l].astype(f32), N_HEADS)[None, :]
        proj = _in_proj(x.reshape(b * s, d), norm1_g[l][None, :], w_in[l], seg, qg, kg)
        proj3 = proj.reshape(b, s, IN_WIDTH)
        bound = _score_bound(q_norm_g[l].astype(f32), k_norm_g[l].astype(f32), rel_bias[l].astype(f32))
        bounded = bound <= SCORE_BOUND
        in_band, tab = _band_bias_table(rel_bias[l])
        tab = jnp.where(in_band, tab - jnp.where(bounded, bound, 0.0), NEG_BIG)
        out_a = _attn_a(proj3, tab, bounded.astype(jnp.int32).reshape(1))
        out_b = _attn_b(proj3)
        x = _mix_ffn(x, out_a, out_b, proj3, w_branch_a[l], w_branch_b[l], w_out[l],
                     norm2_g[l][None, :], w_ffn_up[l].astype(bf16),
                     ffn_conv_w[l], ffn_conv_b[l][None, :], w_ffn_down[l])
    return x
```

```python
import math

import jax
import jax.numpy as jnp
import numpy as np
from jax import lax
from jax.experimental import pallas as pl
from jax.experimental.pallas import tpu as pltpu

D_MODEL = 1024
CHUNK = 64
LEFT_CHUNKS = 8
HEAD_DIM = 64
N_HEADS = 8
WIDTH = N_HEADS * HEAD_DIM
MAX_REL = 128
D_FF = 2816
CONV_WIDTH = 3
EPS = 1e-6

LANES = 128
SUBLANES = 8
MXU_DIM = 256
HEADS_PER_BLOCK = LANES // HEAD_DIM
N_PAIRS = N_HEADS // HEADS_PER_BLOCK
IN_WIDTH = 6 * WIDTH + 2 * D_MODEL

COL_QA, COL_KA, COL_VA = 0, WIDTH // LANES, 2 * WIDTH // LANES
COL_QB, COL_KB, COL_VB = 3 * WIDTH // LANES, 4 * WIDTH // LANES, 5 * WIDTH // LANES

ROW_TILE = 512
PROJ_ROWS = 1024
PROJ_CHUNK = 512
FFN_CHUNK = 256

QA_BLOCK = 128
BAND = LEFT_CHUNKS * CHUNK + QA_BLOCK
PAD = LEFT_CHUNKS * CHUNK
Q_TILE = 2048
A_UNROLL = 16
SCORE_BOUND = 60.0

QB_BLOCK = 256
KB_BLOCK = QB_BLOCK
LOG2E = math.log2(math.e)
DEAD_LOG2 = -151.0
NEG_BIG = -1e30

VMEM_LIMIT = 56 * 1024 * 1024

f32 = jnp.float32
bf16 = jnp.bfloat16


def _resident(shape, index_map):
    return pl.BlockSpec(shape, index_map, pipeline_mode=pl.Buffered(1))


def _nt_dot(a, b):
    return lax.dot_general(a, b, (((1,), (1,)), ((), ())), preferred_element_type=f32)


def _dot(a, b):
    return jnp.dot(a, b, preferred_element_type=f32)


def _sigmoid(x):
    return 1.0 / (1.0 + jnp.exp(-x))


def _stack_heads(q2, lane):
    return jnp.concatenate(
        [jnp.where((lane >= h * HEAD_DIM) & (lane < (h + 1) * HEAD_DIM), q2, jnp.zeros_like(q2))
         for h in range(HEADS_PER_BLOCK)], axis=0)


def _in_proj_kernel(x_ref, g_ref, w_ref, seg_ref, qg_ref, kg_ref, o_ref):
    x = x_ref[...]
    ms = jnp.mean(x * x, axis=-1, keepdims=True)
    hn = (x * lax.rsqrt(ms + EPS) * g_ref[...]).astype(bf16)
    n_chunks = IN_WIDTH // PROJ_CHUNK
    gate_first = 6 * WIDTH // PROJ_CHUNK
    order = list(range(gate_first, n_chunks)) + [0, 1, 3, 2, 4, 5]
    for c in order:
        cols = slice(c * PROJ_CHUNK, (c + 1) * PROJ_CHUNK)
        acc = _dot(hn, w_ref[:, cols].astype(bf16))
        if c in (0, 1):
            sq = (acc * acc).astype(bf16)
            ss = jnp.concatenate(
                [_dot(sq[:, j:j + MXU_DIM], seg_ref[...]) for j in range(0, PROJ_CHUNK, MXU_DIM)],
                axis=1)
            gain = qg_ref[...] if c == 0 else kg_ref[...]
            acc = acc * lax.rsqrt(ss * (1.0 / HEAD_DIM) + EPS) * gain
        if c in (0, 3):
            acc = acc * (HEAD_DIM ** -0.5 * LOG2E)
        if c >= gate_first:
            acc = _sigmoid(acc)
        o_ref[:, cols] = acc.astype(bf16)


def _in_proj(x2d, norm_g, w_in, seg, qg, kg):
    t = x2d.shape[0]
    return pl.pallas_call(
        _in_proj_kernel,
        grid=(t // PROJ_ROWS,),
        in_specs=[
            pl.BlockSpec((PROJ_ROWS, D_MODEL), lambda i: (i, 0)),
            _resident((1, D_MODEL), lambda i: (0, 0)),
            _resident((D_MODEL, IN_WIDTH), lambda i: (0, 0)),
            _resident((MXU_DIM, MXU_DIM), lambda i: (0, 0)),
            _resident((1, WIDTH), lambda i: (0, 0)),
            _resident((1, WIDTH), lambda i: (0, 0)),
        ],
        out_specs=pl.BlockSpec((PROJ_ROWS, IN_WIDTH), lambda i: (i, 0)),
        out_shape=jax.ShapeDtypeStruct((t, IN_WIDTH), bf16),
        compiler_params=pltpu.CompilerParams(
            dimension_semantics=("parallel",), vmem_limit_bytes=VMEM_LIMIT),
        name="in_proj",
    )(x2d, norm_g, w_in, seg, qg, kg)


def _attn_a_kernel(bounded_ref, q_ref, k_ref, v_ref, bias_ref, o_ref, kpad, vpad):
    t = pl.program_id(2)

    @pl.when(t == 0)
    def _():
        zeros = jnp.zeros((PAD, LANES), bf16)
        kpad[0:PAD, :] = zeros
        vpad[0:PAD, :] = zeros
        kpad[PAD:, :] = k_ref[...]
        vpad[PAD:, :] = v_ref[...]

    rows = HEADS_PER_BLOCK * QA_BLOCK
    lane = lax.broadcasted_iota(jnp.int32, (QA_BLOCK, LANES), 1)
    col = lax.broadcasted_iota(jnp.int32, (rows, BAND), 1)
    blocks_per_tile = Q_TILE // QA_BLOCK

    def make_body(band_has_padding, bounded):
        def body(ii, carry):
            qi = t * blocks_per_tile + ii
            r0 = pl.multiple_of(ii * QA_BLOCK, QA_BLOCK)
            b0 = pl.multiple_of(qi * QA_BLOCK, QA_BLOCK)
            qs = _stack_heads(q_ref[pl.ds(r0, QA_BLOCK), :], lane)
            kb = kpad[pl.ds(b0, BAND), :]
            vb = vpad[pl.ds(b0, BAND), :]
            s = _nt_dot(qs, kb) + bias_ref[...].reshape(rows, BAND)
            if band_has_padding:
                s = jnp.where(col < (PAD - qi * QA_BLOCK), NEG_BIG, s)
            if bounded:
                p = jnp.exp2(s)
            else:
                p = jnp.exp2(s - jnp.max(s, axis=-1, keepdims=True))
            l = jnp.sum(p, axis=-1, keepdims=True)
            o = _dot(p.astype(bf16), vb) * (1.0 / l)
            o = jnp.where(lane < HEAD_DIM, o[:QA_BLOCK], o[QA_BLOCK:])
            o_ref[pl.ds(r0, QA_BLOCK), :] = o.astype(bf16)
            return carry
        return body

    padded_blocks = PAD // QA_BLOCK

    def sweep(bounded):
        @pl.when(t == 0)
        def _():
            lax.fori_loop(0, padded_blocks, make_body(True, bounded), 0, unroll=A_UNROLL)
            lax.fori_loop(padded_blocks, blocks_per_tile, make_body(False, bounded), 0,
                          unroll=A_UNROLL)

        @pl.when(t != 0)
        def _():
            lax.fori_loop(0, blocks_per_tile, make_body(False, bounded), 0, unroll=A_UNROLL)

    is_bounded = bounded_ref[0] != 0
    pl.when(is_bounded)(lambda: sweep(True))
    pl.when(jnp.logical_not(is_bounded))(lambda: sweep(False))


def _attn_a(proj3, bias_tab, bounded):
    b, s, _ = proj3.shape
    return pl.pallas_call(
        _attn_a_kernel,
        grid=(b, N_PAIRS, s // Q_TILE),
        in_specs=[
            pl.BlockSpec(memory_space=pltpu.SMEM),
            pl.BlockSpec((None, Q_TILE, LANES), lambda bi, p, t: (bi, t, COL_QA + p)),
            pl.BlockSpec((None, s, LANES), lambda bi, p, t: (bi, 0, COL_KA + p)),
            pl.BlockSpec((None, s, LANES), lambda bi, p, t: (bi, 0, COL_VA + p)),
            pl.BlockSpec((HEADS_PER_BLOCK, QA_BLOCK, BAND), lambda bi, p, t: (p, 0, 0)),
        ],
        out_specs=pl.BlockSpec((None, Q_TILE, LANES), lambda bi, p, t: (bi, t, p)),
        out_shape=jax.ShapeDtypeStruct((b, s, WIDTH), bf16),
        scratch_shapes=[pltpu.VMEM((PAD + s, LANES), bf16),
                        pltpu.VMEM((PAD + s, LANES), bf16)],
        compiler_params=pltpu.CompilerParams(
            dimension_semantics=("parallel", "parallel", "arbitrary"),
            vmem_limit_bytes=VMEM_LIMIT),
        name="attn_a",
    )(bounded, proj3, proj3, proj3, bias_tab)


def _neg_abs(x):
    bits = lax.bitcast_convert_type(x, jnp.uint32) | jnp.uint32(0x80000000)
    return lax.bitcast_convert_type(bits, f32)


def _attn_b_kernel(q_ref, k_ref, v_ref, o_ref, acc_ref, carry_ref):
    t = pl.program_id(2)
    rows = HEADS_PER_BLOCK * QB_BLOCK
    lane = lax.broadcasted_iota(jnp.int32, (QB_BLOCK, LANES), 1)
    row = lax.broadcasted_iota(jnp.int32, (rows, KB_BLOCK), 0)
    col = lax.broadcasted_iota(jnp.int32, (rows, KB_BLOCK), 1)
    causal = col < (row & (QB_BLOCK - 1))
    neg_tri = jnp.where(lax.broadcasted_iota(jnp.int32, (KB_BLOCK, KB_BLOCK), 0)
                        >= lax.broadcasted_iota(jnp.int32, (KB_BLOCK, KB_BLOCK), 1),
                        -1.0, 0.0).astype(bf16)
    blocks_per_tile = Q_TILE // QB_BLOCK

    def scores(qs, k0):
        z = _nt_dot(qs, k_ref[pl.ds(k0, KB_BLOCK), :])
        sp = jnp.maximum(z, 0.0) + jnp.log2(1.0 + jnp.exp2(_neg_abs(z)))
        return z, sp

    def keep_from(sp):
        return _dot(sp.astype(bf16), neg_tri)

    def weights(z, log_keep_from, diag):
        w = jnp.exp2(z + log_keep_from)
        if diag:
            w = jnp.where(causal, w, 0.0)
        return w.astype(bf16), log_keep_from[:, 0:1]

    def alive_of(c):
        return (jnp.max(c) > DEAD_LOG2).astype(jnp.int32)

    def row_start(ii):
        r0 = ii * QB_BLOCK
        return r0 if isinstance(ii, int) else pl.multiple_of(r0, QB_BLOCK)

    def load_q(ii):
        r0 = row_start(ii)
        q0 = pl.multiple_of(t * Q_TILE + ii * QB_BLOCK, QB_BLOCK)
        return q0, _stack_heads(q_ref[pl.ds(r0, QB_BLOCK), :], lane)

    def emit(ii, acc):
        r0 = row_start(ii)
        o = jnp.where(lane < HEAD_DIM, acc[:QB_BLOCK], acc[QB_BLOCK:])
        o_ref[pl.ds(r0, QB_BLOCK), :] = o.astype(bf16)

    def key_tile_pass(lhs, k0, diag_rows):
        z, sp = scores(lhs, k0)
        if diag_rows:
            r = lax.broadcasted_iota(jnp.int32, z.shape, 0)
            c = lax.broadcasted_iota(jnp.int32, z.shape, 1)
            keep = (c < (r & (QB_BLOCK - 1))) | (r >= diag_rows)
            sp = jnp.where(keep, sp, 0.0)
        log_keep_from = keep_from(sp)
        w = jnp.exp2(z + log_keep_from)
        if diag_rows:
            w = jnp.where(keep, w, 0.0)
        return _dot(w.astype(bf16), v_ref[pl.ds(k0, KB_BLOCK), :]), log_keep_from[:, 0:1]

    loaded = [load_q(ii) for ii in range(blocks_per_tile)]
    q_first, qs_first = loaded[0]
    has_prev = q_first > 0
    k_before = pl.multiple_of(jnp.maximum(q_first - KB_BLOCK, 0), KB_BLOCK)
    pv_prev, tot_prev = key_tile_pass(
        jnp.where(has_prev, qs_first, jnp.zeros_like(qs_first)), k_before, 0)
    for j, (q0, qs) in enumerate(loaded):
        last = j == blocks_per_tile - 1
        lhs = qs if last else jnp.concatenate([qs, loaded[j + 1][1]], axis=0)
        pv, tot = key_tile_pass(lhs, q0, rows)
        tot_d = tot[:rows]
        scale = jnp.exp2(tot_d)
        if j == 0:
            scale = jnp.where(has_prev, scale, 0.0)
        acc = pv[:rows] + pv_prev * scale
        acc_ref[j] = acc
        carry_ref[j] = tot_d + tot_prev
        emit(j, acc)
        if not last:
            pv_prev, tot_prev = pv[rows:], tot[rows:]

    def older_blocks(ii, loop_carry):
        q0, qs = load_q(ii)

        def cond(st):
            j, _, _, alive = st
            return jnp.logical_and(j >= 0, alive > 0)

        def sweep(st):
            j, c, a, _ = st
            k0 = pl.multiple_of(j * KB_BLOCK, KB_BLOCK)
            z, sp = scores(qs, k0)
            w, tot = weights(z, keep_from(sp), False)
            a = a + _dot(w, v_ref[pl.ds(k0, KB_BLOCK), :]) * jnp.exp2(c)
            c = c + tot
            return j - 1, c, a, alive_of(c)

        carry = carry_ref[ii]
        _, _, acc, _ = lax.while_loop(
            cond, sweep, (q0 // KB_BLOCK - 2, carry, acc_ref[ii], alive_of(carry)))
        emit(ii, acc)
        return loop_carry

    @pl.when(jnp.max(carry_ref[...]) > DEAD_LOG2)
    def _():
        lax.fori_loop(0, blocks_per_tile, older_blocks, 0)


def _attn_b(proj3):
    b, s, _ = proj3.shape
    return pl.pallas_call(
        _attn_b_kernel,
        grid=(b, N_PAIRS, s // Q_TILE),
        in_specs=[
            pl.BlockSpec((None, Q_TILE, LANES), lambda bi, p, t: (bi, t, COL_QB + p)),
            pl.BlockSpec((None, s, LANES), lambda bi, p, t: (bi, 0, COL_KB + p)),
            pl.BlockSpec((None, s, LANES), lambda bi, p, t: (bi, 0, COL_VB + p)),
        ],
        out_specs=pl.BlockSpec((None, Q_TILE, LANES), lambda bi, p, t: (bi, t, p)),
        out_shape=jax.ShapeDtypeStruct((b, s, WIDTH), bf16),
        scratch_shapes=[
            pltpu.VMEM((Q_TILE // QB_BLOCK, HEADS_PER_BLOCK * QB_BLOCK, LANES), f32),
            pltpu.VMEM((Q_TILE // QB_BLOCK, HEADS_PER_BLOCK * QB_BLOCK, 1), f32)],
        compiler_params=pltpu.CompilerParams(
            dimension_semantics=("parallel", "parallel", "arbitrary"),
            vmem_limit_bytes=VMEM_LIMIT),
        name="attn_b",
    )(proj3, proj3, proj3)


def _mix_ffn_kernel(x_ref, oa_ref, ob_ref, ga_ref, gb_ref, wa_ref, wb_ref, wo_ref, g_ref,
                    wup_ref, cw_ref, cb_ref, wdn_ref, o_ref, tail_ref, hbuf, act_ref):
    @pl.when(pl.program_id(1) == 0)
    def _():
        tail_ref[...] = jnp.zeros_like(tail_ref)

    ya = _dot(oa_ref[...], wa_ref[...].astype(bf16))
    yb = _dot(ob_ref[...], wb_ref[...].astype(bf16))
    mixed = ga_ref[...].astype(f32) * ya + gb_ref[...].astype(f32) * yb
    x = x_ref[...] + _dot(mixed.astype(bf16), wo_ref[...].astype(bf16))
    o_ref[...] = x
    ms = jnp.mean(x * x, axis=-1, keepdims=True)
    hn = (x * lax.rsqrt(ms + EPS) * g_ref[...]).astype(bf16)
    for c in range(D_FF // FFN_CHUNK):
        halves = []
        for half in range(2):
            c0 = half * D_FF + c * FFN_CHUNK
            cols = slice(c0, c0 + FFN_CHUNK)
            h = _dot(hn, wup_ref[:, cols])
            hbuf[half, 0:SUBLANES, :] = tail_ref[:, cols]
            hbuf[half, SUBLANES:, :] = h
            tail_ref[:, cols] = h[ROW_TILE - SUBLANES:, :]
            y = cb_ref[:, cols] + cw_ref[CONV_WIDTH - 1:CONV_WIDTH, cols] * h
            for back in range(1, CONV_WIDTH):
                shifted = hbuf[half, SUBLANES - back:SUBLANES - back + ROW_TILE, :]
                y = y + cw_ref[CONV_WIDTH - 1 - back:CONV_WIDTH - back, cols] * shifted
            halves.append(y)
        gate, up = halves
        act_ref[:, c * FFN_CHUNK:(c + 1) * FFN_CHUNK] = (gate * _sigmoid(gate) * up).astype(bf16)
    o_ref[...] += _dot(act_ref[...], wdn_ref[...].astype(bf16))


def _mix_ffn(x3, oa, ob, proj3, wa, wb, wo, norm_g, w_up, conv_w, conv_b, w_down):
    b, s, _ = x3.shape
    gate_a_blk = 6 * WIDTH // D_MODEL
    row_spec = lambda w, blk: pl.BlockSpec((None, ROW_TILE, w), lambda bi, i: (bi, i, blk))
    const = lambda shape: _resident(shape, lambda bi, i: (0, 0))
    return pl.pallas_call(
        _mix_ffn_kernel,
        grid=(b, s // ROW_TILE),
        in_specs=[
            row_spec(D_MODEL, 0), row_spec(WIDTH, 0), row_spec(WIDTH, 0),
            row_spec(D_MODEL, gate_a_blk), row_spec(D_MODEL, gate_a_blk + 1),
            const((WIDTH, D_MODEL)), const((WIDTH, D_MODEL)), const((D_MODEL, D_MODEL)),
            const((1, D_MODEL)), const((D_MODEL, 2 * D_FF)), const((CONV_WIDTH, 2 * D_FF)),
            const((1, 2 * D_FF)), const((D_FF, D_MODEL)),
        ],
        out_specs=row_spec(D_MODEL, 0),
        out_shape=jax.ShapeDtypeStruct((b, s, D_MODEL), f32),
        scratch_shapes=[pltpu.VMEM((SUBLANES, 2 * D_FF), f32),
                        pltpu.VMEM((2, SUBLANES + ROW_TILE, FFN_CHUNK), f32),
                        pltpu.VMEM((ROW_TILE, D_FF), bf16)],
        compiler_params=pltpu.CompilerParams(
            dimension_semantics=("parallel", "arbitrary"), vmem_limit_bytes=VMEM_LIMIT),
        name="mix_ffn",
    )(x3, oa, ob, proj3, proj3, wa, wb, wo, norm_g, w_up, conv_w, conv_b, w_down)


def _band_bias_table(rel_bias):
    h = rel_bias.shape[0]
    period = BAND + QA_BLOCK
    far = rel_bias[:, 2 * MAX_REL:]
    by_offset = jnp.concatenate([
        jnp.broadcast_to(far, (h, PAD - MAX_REL + 1)),
        rel_bias[:, 2 * MAX_REL - 1:0:-1],
        jnp.zeros((h, 1), rel_bias.dtype),
        jnp.broadcast_to(far, (h, QA_BLOCK - 1)),
    ], axis=1).astype(f32)
    tab = jnp.tile(by_offset, (1, QA_BLOCK))[:, :QA_BLOCK * (period - 1)]
    tab = tab.reshape(h, QA_BLOCK, period - 1)[:, :, :BAND]
    q_chunk = np.arange(QA_BLOCK)[:, None] // CHUNK
    k_chunk = np.floor_divide(np.arange(BAND)[None, :] - PAD, CHUNK)
    in_band = (k_chunk <= q_chunk) & (k_chunk >= q_chunk - LEFT_CHUNKS)
    return jnp.asarray(in_band)[None], tab * LOG2E


def _score_bound(q_gain, k_gain, rel_bias):
    dot_bound = HEAD_DIM ** 0.5 * jnp.max(jnp.abs(q_gain)) * jnp.max(jnp.abs(k_gain))
    return LOG2E * (1.02 * dot_bound + jnp.max(jnp.abs(rel_bias)))


def kernel(x, norm1_g, w_in, q_norm_g, k_norm_g, rel_bias, w_branch_a, w_branch_b, w_out,
           norm2_g, w_ffn_up, ffn_conv_w, ffn_conv_b, w_ffn_down):
    b, s, d = x.shape
    depth = w_in.shape[0]
    head_of = np.arange(MXU_DIM) // HEAD_DIM
    seg = jnp.asarray(head_of[:, None] == head_of[None, :], dtype=bf16)
    for l in range(depth):
        qg = jnp.tile(q_norm_g[l].astype(f32), N_HEADS)[None, :]
        kg = jnp.tile(k_norm_g[l].astype(f32), N_HEADS)[None, :]
        proj = _in_proj(x.reshape(b * s, d), norm1_g[l][None, :], w_in[l], seg, qg, kg)
        proj3 = proj.reshape(b, s, IN_WIDTH)
        bound = _score_bound(q_norm_g[l].astype(f32), k_norm_g[l].astype(f32), rel_bias[l].astype(f32))
        bounded = bound <= SCORE_BOUND
        in_band, tab = _band_bias_table(rel_bias[l])
        tab = jnp.where(in_band, tab - jnp.where(bounded, bound, 0.0), NEG_BIG)
        out_a = _attn_a(proj3, tab, bounded.astype(jnp.int32).reshape(1))
        out_b = _attn_b(proj3)
        x = _mix_ffn(x, out_a, out_b, proj3, w_branch_a[l], w_branch_b[l], w_out[l],
                     norm2_g[l][None, :], w_ffn_up[l].astype(bf16),
                     ffn_conv_w[l], ffn_conv_b[l][None, :], w_ffn_down[l])
    return x
```

```python
import math

import jax
import jax.numpy as jnp
import numpy as np
from jax import lax
from jax.experimental import pallas as pl
from jax.experimental.pallas import tpu as pltpu

D_MODEL = 1024
CHUNK = 64
LEFT_CHUNKS = 8
HEAD_DIM = 64
N_HEADS = 8
WIDTH = N_HEADS * HEAD_DIM
MAX_REL = 128
D_FF = 2816
CONV_WIDTH = 3
EPS = 1e-6

LANES = 128
SUBLANES = 8
MXU_DIM = 256
HEADS_PER_BLOCK = LANES // HEAD_DIM
N_PAIRS = N_HEADS // HEADS_PER_BLOCK
IN_WIDTH = 6 * WIDTH + 2 * D_MODEL

COL_QA, COL_KA, COL_VA = 0, WIDTH // LANES, 2 * WIDTH // LANES
COL_QB, COL_KB, COL_VB = 3 * WIDTH // LANES, 4 * WIDTH // LANES, 5 * WIDTH // LANES

ROW_TILE = 512
PROJ_ROWS = 1024
PROJ_CHUNK = 512
FFN_CHUNK = 256

QA_BLOCK = 128
BAND = LEFT_CHUNKS * CHUNK + QA_BLOCK
PAD = LEFT_CHUNKS * CHUNK
Q_TILE = 4096
A_UNROLL = 16
SCORE_BOUND = 60.0

QB_BLOCK = 256
KB_BLOCK = QB_BLOCK
LOG2E = math.log2(math.e)
DEAD_LOG2 = -151.0
NEG_BIG = -1e30

VMEM_LIMIT = 56 * 1024 * 1024

f32 = jnp.float32
bf16 = jnp.bfloat16


def _resident(shape, index_map):
    return pl.BlockSpec(shape, index_map, pipeline_mode=pl.Buffered(1))


def _nt_dot(a, b):
    return lax.dot_general(a, b, (((1,), (1,)), ((), ())), preferred_element_type=f32)


def _dot(a, b):
    return jnp.dot(a, b, preferred_element_type=f32)


def _sigmoid(x):
    return 1.0 / (1.0 + jnp.exp(-x))


def _stack_heads(q2, lane):
    return jnp.concatenate(
        [jnp.where((lane >= h * HEAD_DIM) & (lane < (h + 1) * HEAD_DIM), q2, jnp.zeros_like(q2))
         for h in range(HEADS_PER_BLOCK)], axis=0)


def _in_proj_kernel(x_ref, g_ref, w_ref, seg_ref, qg_ref, kg_ref, o_ref):
    x = x_ref[...]
    ms = jnp.mean(x * x, axis=-1, keepdims=True)
    hn = (x * lax.rsqrt(ms + EPS) * g_ref[...]).astype(bf16)
    n_chunks = IN_WIDTH // PROJ_CHUNK
    gate_first = 6 * WIDTH // PROJ_CHUNK
    order = list(range(gate_first, n_chunks)) + [0, 1, 3, 2, 4, 5]
    for c in order:
        cols = slice(c * PROJ_CHUNK, (c + 1) * PROJ_CHUNK)
        acc = _dot(hn, w_ref[:, cols].astype(bf16))
        if c in (0, 1):
            sq = (acc * acc).astype(bf16)
            ss = jnp.concatenate(
                [_dot(sq[:, j:j + MXU_DIM], seg_ref[...]) for j in range(0, PROJ_CHUNK, MXU_DIM)],
                axis=1)
            gain = qg_ref[...] if c == 0 else kg_ref[...]
            acc = acc * lax.rsqrt(ss * (1.0 / HEAD_DIM) + EPS) * gain
        if c in (0, 3):
            acc = acc * (HEAD_DIM ** -0.5 * LOG2E)
        if c >= gate_first:
            acc = _sigmoid(acc)
        o_ref[:, cols] = acc.astype(bf16)


def _in_proj(x2d, norm_g, w_in, seg, qg, kg):
    t = x2d.shape[0]
    return pl.pallas_call(
        _in_proj_kernel,
        grid=(t // PROJ_ROWS,),
        in_specs=[
            pl.BlockSpec((PROJ_ROWS, D_MODEL), lambda i: (i, 0)),
            _resident((1, D_MODEL), lambda i: (0, 0)),
            _resident((D_MODEL, IN_WIDTH), lambda i: (0, 0)),
            _resident((MXU_DIM, MXU_DIM), lambda i: (0, 0)),
            _resident((1, WIDTH), lambda i: (0, 0)),
            _resident((1, WIDTH), lambda i: (0, 0)),
        ],
        out_specs=pl.BlockSpec((PROJ_ROWS, IN_WIDTH), lambda i: (i, 0)),
        out_shape=jax.ShapeDtypeStruct((t, IN_WIDTH), bf16),
        compiler_params=pltpu.CompilerParams(
            dimension_semantics=("parallel",), vmem_limit_bytes=VMEM_LIMIT),
        name="in_proj",
    )(x2d, norm_g, w_in, seg, qg, kg)


def _attn_a_kernel(bounded_ref, q_ref, k_ref, v_ref, bias_ref, o_ref, kpad, vpad):
    t = pl.program_id(2)

    @pl.when(t == 0)
    def _():
        zeros = jnp.zeros((PAD, LANES), bf16)
        kpad[0:PAD, :] = zeros
        vpad[0:PAD, :] = zeros
        kpad[PAD:, :] = k_ref[...]
        vpad[PAD:, :] = v_ref[...]

    rows = HEADS_PER_BLOCK * QA_BLOCK
    lane = lax.broadcasted_iota(jnp.int32, (QA_BLOCK, LANES), 1)
    col = lax.broadcasted_iota(jnp.int32, (rows, BAND), 1)
    blocks_per_tile = Q_TILE // QA_BLOCK

    def make_body(band_has_padding, bounded):
        def body(ii, carry):
            qi = t * blocks_per_tile + ii
            r0 = pl.multiple_of(ii * QA_BLOCK, QA_BLOCK)
            b0 = pl.multiple_of(qi * QA_BLOCK, QA_BLOCK)
            qs = _stack_heads(q_ref[pl.ds(r0, QA_BLOCK), :], lane)
            kb = kpad[pl.ds(b0, BAND), :]
            vb = vpad[pl.ds(b0, BAND), :]
            s = _nt_dot(qs, kb) + bias_ref[...].reshape(rows, BAND)
            if band_has_padding:
                s = jnp.where(col < (PAD - qi * QA_BLOCK), NEG_BIG, s)
            if bounded:
                p = jnp.exp2(s)
            else:
                p = jnp.exp2(s - jnp.max(s, axis=-1, keepdims=True))
            l = jnp.sum(p, axis=-1, keepdims=True)
            o = _dot(p.astype(bf16), vb) * (1.0 / l)
            o = jnp.where(lane < HEAD_DIM, o[:QA_BLOCK], o[QA_BLOCK:])
            o_ref[pl.ds(r0, QA_BLOCK), :] = o.astype(bf16)
            return carry
        return body

    padded_blocks = PAD // QA_BLOCK

    def sweep(bounded):
        @pl.when(t == 0)
        def _():
            lax.fori_loop(0, padded_blocks, make_body(True, bounded), 0, unroll=A_UNROLL)
            lax.fori_loop(padded_blocks, blocks_per_tile, make_body(False, bounded), 0,
                          unroll=A_UNROLL)

        @pl.when(t != 0)
        def _():
            lax.fori_loop(0, blocks_per_tile, make_body(False, bounded), 0, unroll=A_UNROLL)

    is_bounded = bounded_ref[0] != 0
    pl.when(is_bounded)(lambda: sweep(True))
    pl.when(jnp.logical_not(is_bounded))(lambda: sweep(False))


def _attn_a(proj3, bias_tab, bounded):
    b, s, _ = proj3.shape
    return pl.pallas_call(
        _attn_a_kernel,
        grid=(b, N_PAIRS, s // Q_TILE),
        in_specs=[
            pl.BlockSpec(memory_space=pltpu.SMEM),
            pl.BlockSpec((None, Q_TILE, LANES), lambda bi, p, t: (bi, t, COL_QA + p)),
            pl.BlockSpec((None, s, LANES), lambda bi, p, t: (bi, 0, COL_KA + p)),
            pl.BlockSpec((None, s, LANES), lambda bi, p, t: (bi, 0, COL_VA + p)),
            pl.BlockSpec((HEADS_PER_BLOCK, QA_BLOCK, BAND), lambda bi, p, t: (p, 0, 0)),
        ],
        out_specs=pl.BlockSpec((None, Q_TILE, LANES), lambda bi, p, t: (bi, t, p)),
        out_shape=jax.ShapeDtypeStruct((b, s, WIDTH), bf16),
        scratch_shapes=[pltpu.VMEM((PAD + s, LANES), bf16),
                        pltpu.VMEM((PAD + s, LANES), bf16)],
        compiler_params=pltpu.CompilerParams(
            dimension_semantics=("parallel", "parallel", "arbitrary"),
            vmem_limit_bytes=VMEM_LIMIT),
        name="attn_a",
    )(bounded, proj3, proj3, proj3, bias_tab)


def _neg_abs(x):
    bits = lax.bitcast_convert_type(x, jnp.uint32) | jnp.uint32(0x80000000)
    return lax.bitcast_convert_type(bits, f32)


def _attn_b_kernel(q_ref, k_ref, v_ref, o_ref, acc_ref, carry_ref):
    t = pl.program_id(2)
    rows = HEADS_PER_BLOCK * QB_BLOCK
    lane = lax.broadcasted_iota(jnp.int32, (QB_BLOCK, LANES), 1)
    row = lax.broadcasted_iota(jnp.int32, (rows, KB_BLOCK), 0)
    col = lax.broadcasted_iota(jnp.int32, (rows, KB_BLOCK), 1)
    causal = col < (row & (QB_BLOCK - 1))
    neg_tri = jnp.where(lax.broadcasted_iota(jnp.int32, (KB_BLOCK, KB_BLOCK), 0)
                        >= lax.broadcasted_iota(jnp.int32, (KB_BLOCK, KB_BLOCK), 1),
                        -1.0, 0.0).astype(bf16)
    blocks_per_tile = Q_TILE // QB_BLOCK

    def scores(qs, k0):
        z = _nt_dot(qs, k_ref[pl.ds(k0, KB_BLOCK), :])
        sp = jnp.maximum(z, 0.0) + jnp.log2(1.0 + jnp.exp2(_neg_abs(z)))
        return z, sp

    def keep_from(sp):
        return _dot(sp.astype(bf16), neg_tri)

    def weights(z, log_keep_from, diag):
        w = jnp.exp2(z + log_keep_from)
        if diag:
            w = jnp.where(causal, w, 0.0)
        return w.astype(bf16), log_keep_from[:, 0:1]

    def alive_of(c):
        return (jnp.max(c) > DEAD_LOG2).astype(jnp.int32)

    def row_start(ii):
        r0 = ii * QB_BLOCK
        return r0 if isinstance(ii, int) else pl.multiple_of(r0, QB_BLOCK)

    def load_q(ii):
        r0 = row_start(ii)
        q0 = pl.multiple_of(t * Q_TILE + ii * QB_BLOCK, QB_BLOCK)
        return q0, _stack_heads(q_ref[pl.ds(r0, QB_BLOCK), :], lane)

    def emit(ii, acc):
        r0 = row_start(ii)
        o = jnp.where(lane < HEAD_DIM, acc[:QB_BLOCK], acc[QB_BLOCK:])
        o_ref[pl.ds(r0, QB_BLOCK), :] = o.astype(bf16)

    def key_tile_pass(lhs, k0, diag_rows):
        z, sp = scores(lhs, k0)
        if diag_rows:
            r = lax.broadcasted_iota(jnp.int32, z.shape, 0)
            c = lax.broadcasted_iota(jnp.int32, z.shape, 1)
            keep = (c < (r & (QB_BLOCK - 1))) | (r >= diag_rows)
            sp = jnp.where(keep, sp, 0.0)
        log_keep_from = keep_from(sp)
        w = jnp.exp2(z + log_keep_from)
        if diag_rows:
            w = jnp.where(keep, w, 0.0)
        return _dot(w.astype(bf16), v_ref[pl.ds(k0, KB_BLOCK), :]), log_keep_from[:, 0:1]

    loaded = [load_q(ii) for ii in range(blocks_per_tile)]
    q_first, qs_first = loaded[0]
    has_prev = q_first > 0
    k_before = pl.multiple_of(jnp.maximum(q_first - KB_BLOCK, 0), KB_BLOCK)
    pv_prev, tot_prev = key_tile_pass(
        jnp.where(has_prev, qs_first, jnp.zeros_like(qs_first)), k_before, 0)
    for j, (q0, qs) in enumerate(loaded):
        last = j == blocks_per_tile - 1
        lhs = qs if last else jnp.concatenate([qs, loaded[j + 1][1]], axis=0)
        pv, tot = key_tile_pass(lhs, q0, rows)
        tot_d = tot[:rows]
        scale = jnp.exp2(tot_d)
        if j == 0:
            scale = jnp.where(has_prev, scale, 0.0)
        acc = pv[:rows] + pv_prev * scale
        acc_ref[j] = acc
        carry_ref[j] = tot_d + tot_prev
        emit(j, acc)
        if not last:
            pv_prev, tot_prev = pv[rows:], tot[rows:]

    def older_blocks(ii, loop_carry):
        q0, qs = load_q(ii)

        def cond(st):
            j, _, _, alive = st
            return jnp.logical_and(j >= 0, alive > 0)

        def sweep(st):
            j, c, a, _ = st
            k0 = pl.multiple_of(j * KB_BLOCK, KB_BLOCK)
            z, sp = scores(qs, k0)
            w, tot = weights(z, keep_from(sp), False)
            a = a + _dot(w, v_ref[pl.ds(k0, KB_BLOCK), :]) * jnp.exp2(c)
            c = c + tot
            return j - 1, c, a, alive_of(c)

        carry = carry_ref[ii]
        _, _, acc, _ = lax.while_loop(
            cond, sweep, (q0 // KB_BLOCK - 2, carry, acc_ref[ii], alive_of(carry)))
        emit(ii, acc)
        return loop_carry

    @pl.when(jnp.max(carry_ref[...]) > DEAD_LOG2)
    def _():
        lax.fori_loop(0, blocks_per_tile, older_blocks, 0)


def _attn_b(proj3):
    b, s, _ = proj3.shape
    return pl.pallas_call(
        _attn_b_kernel,
        grid=(b, N_PAIRS, s // Q_TILE),
        in_specs=[
            pl.BlockSpec((None, Q_TILE, LANES), lambda bi, p, t: (bi, t, COL_QB + p)),
            pl.BlockSpec((None, s, LANES), lambda bi, p, t: (bi, 0, COL_KB + p)),
            pl.BlockSpec((None, s, LANES), lambda bi, p, t: (bi, 0, COL_VB + p)),
        ],
        out_specs=pl.BlockSpec((None, Q_TILE, LANES), lambda bi, p, t: (bi, t, p)),
        out_shape=jax.ShapeDtypeStruct((b, s, WIDTH), bf16),
        scratch_shapes=[
            pltpu.VMEM((Q_TILE // QB_BLOCK, HEADS_PER_BLOCK * QB_BLOCK, LANES), f32),
            pltpu.VMEM((Q_TILE // QB_BLOCK, HEADS_PER_BLOCK * QB_BLOCK, 1), f32)],
        compiler_params=pltpu.CompilerParams(
            dimension_semantics=("parallel", "parallel", "arbitrary"),
            vmem_limit_bytes=VMEM_LIMIT),
        name="attn_b",
    )(proj3, proj3, proj3)


def _mix_ffn_kernel(x_ref, oa_ref, ob_ref, ga_ref, gb_ref, wa_ref, wb_ref, wo_ref, g_ref,
                    wup_ref, cw_ref, cb_ref, wdn_ref, o_ref, tail_ref, hbuf, act_ref):
    @pl.when(pl.program_id(1) == 0)
    def _():
        tail_ref[...] = jnp.zeros_like(tail_ref)

    ya = _dot(oa_ref[...], wa_ref[...].astype(bf16))
    yb = _dot(ob_ref[...], wb_ref[...].astype(bf16))
    mixed = ga_ref[...].astype(f32) * ya + gb_ref[...].astype(f32) * yb
    x = x_ref[...] + _dot(mixed.astype(bf16), wo_ref[...].astype(bf16))
    o_ref[...] = x
    ms = jnp.mean(x * x, axis=-1, keepdims=True)
    hn = (x * lax.rsqrt(ms + EPS) * g_ref[...]).astype(bf16)
    for c in range(D_FF // FFN_CHUNK):
        halves = []
        for half in range(2):
            c0 = half * D_FF + c * FFN_CHUNK
            cols = slice(c0, c0 + FFN_CHUNK)
            h = _dot(hn, wup_ref[:, cols])
            hbuf[half, 0:SUBLANES, :] = tail_ref[:, cols]
            hbuf[half, SUBLANES:, :] = h
            tail_ref[:, cols] = h[ROW_TILE - SUBLANES:, :]
            y = cb_ref[:, cols] + cw_ref[CONV_WIDTH - 1:CONV_WIDTH, cols] * h
            for back in range(1, CONV_WIDTH):
                shifted = hbuf[half, SUBLANES - back:SUBLANES - back + ROW_TILE, :]
                y = y + cw_ref[CONV_WIDTH - 1 - back:CONV_WIDTH - back, cols] * shifted
            halves.append(y)
        gate, up = halves
        act_ref[:, c * FFN_CHUNK:(c + 1) * FFN_CHUNK] = (gate * _sigmoid(gate) * up).astype(bf16)
    o_ref[...] += _dot(act_ref[...], wdn_ref[...].astype(bf16))


def _mix_ffn(x3, oa, ob, proj3, wa, wb, wo, norm_g, w_up, conv_w, conv_b, w_down):
    b, s, _ = x3.shape
    gate_a_blk = 6 * WIDTH // D_MODEL
    row_spec = lambda w, blk: pl.BlockSpec((None, ROW_TILE, w), lambda bi, i: (bi, i, blk))
    const = lambda shape: _resident(shape, lambda bi, i: (0, 0))
    return pl.pallas_call(
        _mix_ffn_kernel,
        grid=(b, s // ROW_TILE),
        in_specs=[
            row_spec(D_MODEL, 0), row_spec(WIDTH, 0), row_spec(WIDTH, 0),
            row_spec(D_MODEL, gate_a_blk), row_spec(D_MODEL, gate_a_blk + 1),
            const((WIDTH, D_MODEL)), const((WIDTH, D_MODEL)), const((D_MODEL, D_MODEL)),
            const((1, D_MODEL)), const((D_MODEL, 2 * D_FF)), const((CONV_WIDTH, 2 * D_FF)),
            const((1, 2 * D_FF)), const((D_FF, D_MODEL)),
        ],
        out_specs=row_spec(D_MODEL, 0),
        out_shape=jax.ShapeDtypeStruct((b, s, D_MODEL), f32),
        scratch_shapes=[pltpu.VMEM((SUBLANES, 2 * D_FF), f32),
                        pltpu.VMEM((2, SUBLANES + ROW_TILE, FFN_CHUNK), f32),
                        pltpu.VMEM((ROW_TILE, D_FF), bf16)],
        compiler_params=pltpu.CompilerParams(
            dimension_semantics=("parallel", "arbitrary"), vmem_limit_bytes=VMEM_LIMIT),
        name="mix_ffn",
    )(x3, oa, ob, proj3, proj3, wa, wb, wo, norm_g, w_up, conv_w, conv_b, w_down)


def _band_bias_table(rel_bias):
    h = rel_bias.shape[0]
    period = BAND + QA_BLOCK
    far = rel_bias[:, 2 * MAX_REL:]
    by_offset = jnp.concatenate([
        jnp.broadcast_to(far, (h, PAD - MAX_REL + 1)),
        rel_bias[:, 2 * MAX_REL - 1:0:-1],
        jnp.zeros((h, 1), rel_bias.dtype),
        jnp.broadcast_to(far, (h, QA_BLOCK - 1)),
    ], axis=1).astype(f32)
    tab = jnp.tile(by_offset, (1, QA_BLOCK))[:, :QA_BLOCK * (period - 1)]
    tab = tab.reshape(h, QA_BLOCK, period - 1)[:, :, :BAND]
    q_chunk = np.arange(QA_BLOCK)[:, None] // CHUNK
    k_chunk = np.floor_divide(np.arange(BAND)[None, :] - PAD, CHUNK)
    in_band = (k_chunk <= q_chunk) & (k_chunk >= q_chunk - LEFT_CHUNKS)
    return jnp.asarray(in_band)[None], tab * LOG2E


def _score_bound(q_gain, k_gain, rel_bias):
    dot_bound = HEAD_DIM ** 0.5 * jnp.max(jnp.abs(q_gain)) * jnp.max(jnp.abs(k_gain))
    return LOG2E * (1.02 * dot_bound + jnp.max(jnp.abs(rel_bias)))


def kernel(x, norm1_g, w_in, q_norm_g, k_norm_g, rel_bias, w_branch_a, w_branch_b, w_out,
           norm2_g, w_ffn_up, ffn_conv_w, ffn_conv_b, w_ffn_down):
    b, s, d = x.shape
    depth = w_in.shape[0]
    head_of = np.arange(MXU_DIM) // HEAD_DIM
    seg = jnp.asarray(head_of[:, None] == head_of[None, :], dtype=bf16)
    for l in range(depth):
        qg = jnp.tile(q_norm_g[l].astype(f32), N_HEADS)[None, :]
        kg = jnp.tile(k_norm_g[l].astype(f32), N_HEADS)[None, :]
        proj = _in_proj(x.reshape(b * s, d), norm1_g[l][None, :], w_in[l], seg, qg, kg)
        proj3 = proj.reshape(b, s, IN_WIDTH)
        bound = _score_bound(q_norm_g[l].astype(f32), k_norm_g[l].astype(f32), rel_bias[l].astype(f32))
        bounded = bound <= SCORE_BOUND
        in_band, tab = _band_bias_table(rel_bias[l])
        tab = jnp.where(in_band, tab - jnp.where(bounded, bound, 0.0), NEG_BIG)
        out_a = _attn_a(proj3, tab, bounded.astype(jnp.int32).reshape(1))
        out_b = _attn_b(proj3)
        x = _mix_ffn(x, out_a, out_b, proj3, w_branch_a[l], w_branch_b[l], w_out[l],
                     norm2_g[l][None, :], w_ffn_up[l].astype(bf16),
                     ffn_conv_w[l], ffn_conv_b[l][None, :], w_ffn_down[l])
    return x
```
